```python
import math
import jax
import jax.numpy as jnp
from jax import lax
import numpy as np

D_MODEL = 2048
BATCH = 8
SEQ = 8192
DEPTH = 4

MEM_LEN = 256
EPS = 1e-6
GDN_HEADS = 8
GDN_DK = 128
GDN_DV = 128
GDN_CONV = 4
GDN_CHUNK = 64
LRU_WIDTH = D_MODEL // 2
LRU_BLOCKS = 8
LRU_BLOCK = LRU_WIDTH // LRU_BLOCKS
LRU_CONV = 4
LRU_C = 8.0
SWA_HEADS = 32
SWA_KV_HEADS = 8
SWA_GROUP = SWA_HEADS // SWA_KV_HEADS
SWA_HD = 64
SWA_WINDOW = 128
SWA_BLOCK = 128
ROPE_THETA = 10000.0
X_HEADS = 4
X_HD = 128
D_FF = 5632
FFN_CONV = 3

GDN_QK = GDN_HEADS * GDN_DK
GDN_V = GDN_HEADS * GDN_DV
GDN_QKV = 2 * GDN_QK + GDN_V
HYB_SIZES = (GDN_QKV, GDN_V, GDN_HEADS, GDN_HEADS, LRU_WIDTH, LRU_WIDTH)
HYB_IN = GDN_QKV + GDN_V + 2 * GDN_HEADS + 2 * LRU_WIDTH
HYB_MIX = GDN_V + LRU_WIDTH
SWA_Q = SWA_HEADS * SWA_HD
SWA_KV = SWA_KV_HEADS * SWA_HD
SWA_SIZES = (SWA_Q, SWA_KV, SWA_KV)
SWA_IN = SWA_Q + 2 * SWA_KV
X_INNER = X_HEADS * X_HD
N_EVEN = (DEPTH + 1) // 2
N_ODD = DEPTH // 2

kernel_name = 'hybrid_gdn_rglru_swa_trunk'


def split_cols(t, sizes):
    out, start = [], 0
    for s in sizes:
        out.append(t[..., start:start + s])
        start += s
    return out


def rmsnorm(x, g):
    xf = x.astype(jnp.float32)
    y = xf * lax.rsqrt(jnp.mean(xf * xf, axis=-1, keepdims=True) + EPS)
    return (y * g.astype(jnp.float32)).astype(x.dtype)


def causal_dwconv(x, w):
    width, ch = w.shape
    return lax.conv_general_dilated(
        x, w[:, None, :].astype(x.dtype), window_strides=(1,), padding=((width - 1, 0),),
        dimension_numbers=('NWC', 'WIO', 'NWC'), feature_group_count=ch)


def rope_tables(positions, dim):
    inv = 1.0 / (ROPE_THETA ** (jnp.arange(0, dim, 2, dtype=jnp.float32) / dim))
    ang = positions.astype(jnp.float32)[..., None] * inv
    return jnp.cos(ang), jnp.sin(ang)


def apply_rope(x, cos, sin):
    xf = x.astype(jnp.float32)
    x1, x2 = jnp.split(xf, 2, axis=-1)
    c, s = cos[:, :, None, :], sin[:, :, None, :]
    return jnp.concatenate([x1 * c - x2 * s, x2 * c + x1 * s], axis=-1).astype(x.dtype)


def gated_delta_rule(q, k, v, g, beta):
    b, s, h, dk = q.shape
    dv = v.shape[-1]
    c = GDN_CHUNK
    n = s // c

    def blocks(t):
        t = t.astype(jnp.float32).reshape((b, n, c, h) + t.shape[3:])
        return jnp.moveaxis(t, 3, 1)

    q = blocks(q) * (dk ** -0.5)
    k, v, g, beta = blocks(k), blocks(v), blocks(g), blocks(beta)
    gc = jnp.cumsum(g, axis=-1)
    causal = jnp.tril(jnp.ones((c, c), dtype=bool))
    strict = jnp.tril(jnp.ones((c, c), dtype=bool), -1)
    decay = jnp.exp(jnp.where(causal, gc[..., :, None] - gc[..., None, :], -jnp.inf))
    kb = k * beta[..., None]
    l_mat = jnp.where(strict, jnp.einsum('bhnik,bhnjk->bhnij', kb, k) * decay, 0.0)
    eye = jnp.eye(c, dtype=jnp.float32)
    tmat = lax.linalg.triangular_solve(eye + l_mat, jnp.broadcast_to(eye, l_mat.shape),
                                       left_side=True, lower=True, unit_diagonal=True)
    u = jnp.einsum('bhnij,bhnjd->bhnid', tmat, v * beta[..., None])
    w = jnp.einsum('bhnij,bhnjd->bhnid', tmat, kb * jnp.exp(gc)[..., None])
    attn = jnp.where(causal, jnp.einsum('bhnik,bhnjk->bhnij', q, k) * decay, 0.0)
    q_dec = q * jnp.exp(gc)[..., None]
    g_last = gc[..., -1]
    k_dec = k * jnp.exp(g_last[..., None] - gc)[..., None]

    def step(state, inp):
        u_i, w_i, a_i, qd_i, kd_i, gl_i = inp
        v_new = u_i - jnp.einsum('bhck,bhkv->bhcv', w_i, state)
        o_i = jnp.einsum('bhck,bhkv->bhcv', qd_i, state) + jnp.einsum('bhcj,bhjv->bhcv', a_i, v_new)
        state = state * jnp.exp(gl_i)[..., None, None] + jnp.einsum('bhck,bhcv->bhkv', kd_i, v_new)
        return state, o_i

    xs = tuple(jnp.moveaxis(t, 2, 0) for t in (u, w, attn, q_dec, k_dec, g_last))
    state0 = jnp.zeros((b, h, dk, dv), jnp.float32)
    _, o = lax.scan(step, state0, xs)
    return jnp.transpose(o, (1, 0, 3, 2, 4)).reshape(b, s, h, dv)


def gdn_mixer(qkv, z, b_raw, a_raw, conv_w, a_log, dt_bias, norm_g):
    bsz, s, _ = qkv.shape
    qkv_c = jax.nn.silu(causal_dwconv(qkv, conv_w))
    q, k, v = split_cols(qkv_c, (GDN_QK, GDN_QK, GDN_V))

    def heads(t, d):
        return t.reshape(bsz, s, GDN_HEADS, d).astype(jnp.float32)

    def l2n(t):
        return t * lax.rsqrt(jnp.sum(t * t, axis=-1, keepdims=True) + EPS)

    q, k, v = l2n(heads(q, GDN_DK)), l2n(heads(k, GDN_DK)), heads(v, GDN_DV)
    beta = jax.nn.sigmoid(b_raw.astype(jnp.float32))
    g = -jnp.exp(a_log.astype(jnp.float32)) * jax.nn.softplus(
        a_raw.astype(jnp.float32) + dt_bias.astype(jnp.float32))
    o = gated_delta_rule(q, k, v, g, beta)
    o = rmsnorm(o, norm_g) * jax.nn.silu(heads(z, GDN_DV))
    return o.reshape(bsz, s, GDN_V).astype(qkv.dtype)


def rglru_mixer(xr, gate, conv_w, conv_b, w_r, b_r, w_i, b_i, lam):
    bsz, s, _ = xr.shape
    xc = (causal_dwconv(xr, conv_w) + conv_b).astype(jnp.float32)
    xh = xc.reshape(bsz, s, LRU_BLOCKS, LRU_BLOCK)
    r = jax.nn.sigmoid(jnp.einsum('bshi,hij->bshj', xh, w_r.astype(jnp.float32)).reshape(bsz, s, LRU_WIDTH)
                       + b_r.astype(jnp.float32))
    i = jax.nn.sigmoid(jnp.einsum('bshi,hij->bshj', xh, w_i.astype(jnp.float32)).reshape(bsz, s, LRU_WIDTH)
                       + b_i.astype(jnp.float32))
    log_a = -LRU_C * r * jax.nn.softplus(-lam.astype(jnp.float32))
    a = jnp.exp(log_a)
    inp = jnp.sqrt(-jnp.expm1(2.0 * log_a)) * (i * xc)

    def combine(lhs, rhs):
        a_l, h_l = lhs
        a_r, h_r = rhs
        return a_l * a_r, a_r * h_l + h_r

    _, hseq = lax.associative_scan(combine, (a, inp), axis=1)
    return hseq.astype(xr.dtype) * jax.nn.gelu(gate)


def swa_mixer(qkv, q_norm, k_norm, sinks, cos, sin):
    bsz, s, _ = qkv.shape
    q, k, v = split_cols(qkv, SWA_SIZES)
    q = apply_rope(rmsnorm(q.reshape(bsz, s, SWA_HEADS, SWA_HD), q_norm), cos, sin)
    k = apply_rope(rmsnorm(k.reshape(bsz, s, SWA_KV_HEADS, SWA_HD), k_norm), cos, sin)
    v = v.reshape(bsz, s, SWA_KV_HEADS, SWA_HD)
    blk = SWA_BLOCK
    nb = s // blk
    qb = q.reshape(bsz, nb, blk, SWA_KV_HEADS, SWA_GROUP, SWA_HD)

    def band(t):
        tp = jnp.pad(t, ((0, 0), (blk, 0), (0, 0), (0, 0))).reshape(bsz, nb + 1, blk, SWA_KV_HEADS, SWA_HD)
        return jnp.concatenate([tp[:, :-1], tp[:, 1:]], axis=2)

    kb, vb = band(k), band(v)
    sc = jnp.einsum('bnqkgd,bnjkd->bnkgqj', qb, kb).astype(jnp.float32) * (SWA_HD ** -0.5)
    qi = jnp.arange(blk)[:, None]
    kj = jnp.arange(2 * blk)[None, :]
    rel = blk + qi - kj
    bidx = jnp.arange(nb)[:, None, None]
    valid = (rel >= 0) & (rel < SWA_WINDOW) & (bidx * blk + kj - blk >= 0)
    sc = jnp.where(valid[None, :, None, None], sc, -jnp.inf)
    sink = sinks.astype(jnp.float32).reshape(SWA_KV_HEADS, SWA_GROUP)[None, None, :, :, None, None]
    m = jnp.maximum(jnp.max(sc, axis=-1, keepdims=True), sink)
    p = jnp.exp(sc - m)
    denom = jnp.sum(p, axis=-1, keepdims=True) + jnp.exp(sink - m)
    o = jnp.einsum('bnkgqj,bnjkd->bnqkgd', p / denom, vb.astype(jnp.float32))
    return o.reshape(bsz, s, SWA_Q).astype(qkv.dtype)


def cross_attn(h, mem_n, wq, wkv, wo, q_norm, k_norm):
    bsz, s, _ = h.shape
    q = rmsnorm((h @ wq).reshape(bsz, s, X_HEADS, X_HD), q_norm)
    k, v = split_cols(mem_n @ wkv, (X_INNER, X_INNER))
    k = rmsnorm(k.reshape(bsz, MEM_LEN, X_HEADS, X_HD), k_norm)
    v = v.reshape(bsz, MEM_LEN, X_HEADS, X_HD)
    sc = jnp.einsum('bshd,bmhd->bhsm', q, k).astype(jnp.float32) * (X_HD ** -0.5)
    p = jax.nn.softmax(sc, axis=-1).astype(v.dtype)
    o = jnp.einsum('bhsm,bmhd->bshd', p, v).reshape(bsz, s, X_INNER)
    return o @ wo


def conv_ffn(h, w_in, conv_w, conv_b, w_out):
    gt, up = split_cols(h @ w_in, (D_FF, D_FF))
    gt = causal_dwconv(gt, conv_w) + conv_b
    return (jax.nn.silu(gt) * up) @ w_out


def _fwd_setup_inputs(seed: int = 0) -> dict:
    key = jax.random.key(seed)
    ks = iter(jax.random.split(key, 48))
    f32 = jnp.float32
    D, L, E, O = D_MODEL, DEPTH, N_EVEN, N_ODD

    def nrm(shape, scale):
        return jax.random.normal(next(ks), shape, f32) * scale

    def gain(shape):
        return 1.0 + nrm(shape, 0.02)

    x = nrm((BATCH, SEQ, D), 1.0)
    mem = nrm((BATCH, MEM_LEN, D), 1.0)
    positions = (jax.random.randint(next(ks), (BATCH, 1), 0, 1024) + jnp.arange(SEQ)[None, :]).astype(jnp.int32)
    norm_mix = gain((L, D))
    norm_cross = gain((L, D))
    norm_mem = gain((L, D))
    norm_ffn = gain((L, D))
    xq_w = nrm((L, D, X_INNER), D ** -0.5)
    xkv_w = nrm((L, D, 2 * X_INNER), D ** -0.5)
    xo_w = nrm((L, X_INNER, D), X_INNER ** -0.5)
    xq_norm = gain((L, X_HD))
    xk_norm = gain((L, X_HD))
    ffn_in_w = nrm((L, D, 2 * D_FF), D ** -0.5)
    ffn_conv_w = nrm((L, FFN_CONV, D_FF), FFN_CONV ** -0.5)
    ffn_conv_b = nrm((L, D_FF), 0.02)
    ffn_out_w = nrm((L, D_FF, D), D_FF ** -0.5)
    hyb_in_w = nrm((E, D, HYB_IN), D ** -0.5)
    hyb_out_w = nrm((E, HYB_MIX, D), HYB_MIX ** -0.5)
    gdn_conv_w = nrm((E, GDN_CONV, GDN_QKV), GDN_CONV ** -0.5)
    gdn_a_log = jnp.log(jax.random.uniform(next(ks), (E, GDN_HEADS), f32, 1.0, 16.0))
    dt = jnp.exp(jax.random.uniform(next(ks), (E, GDN_HEADS), f32, math.log(1e-3), math.log(1e-1)))
    gdn_dt_bias = dt + jnp.log(-jnp.expm1(-dt))
    gdn_norm = gain((E, GDN_DV))
    lru_conv_w = nrm((E, LRU_CONV, LRU_WIDTH), LRU_CONV ** -0.5)
    lru_conv_b = nrm((E, LRU_WIDTH), 0.02)
    lru_wr = nrm((E, LRU_BLOCKS, LRU_BLOCK, LRU_BLOCK), LRU_BLOCK ** -0.5)
    lru_br = nrm((E, LRU_WIDTH), 0.02)
    lru_wi = nrm((E, LRU_BLOCKS, LRU_BLOCK, LRU_BLOCK), LRU_BLOCK ** -0.5)
    lru_bi = nrm((E, LRU_WIDTH), 0.02)
    a0 = jax.random.uniform(next(ks), (E, LRU_WIDTH), f32, 0.9, 0.999) ** (1.0 / LRU_C)
    lru_lambda = jnp.log(a0) - jnp.log1p(-a0)
    swa_in_w = nrm((O, D, SWA_IN), D ** -0.5)
    swa_out_w = nrm((O, SWA_Q, D), SWA_Q ** -0.5)
    swa_q_norm = gain((O, SWA_HD))
    swa_k_norm = gain((O, SWA_HD))
    swa_sinks = nrm((O, SWA_HEADS), 0.5)
    return {
        'x': x, 'mem': mem, 'positions': positions,
        'norm_mix': norm_mix, 'norm_cross': norm_cross, 'norm_mem': norm_mem, 'norm_ffn': norm_ffn,
        'xq_w': xq_w, 'xkv_w': xkv_w, 'xo_w': xo_w, 'xq_norm': xq_norm, 'xk_norm': xk_norm,
        'ffn_in_w': ffn_in_w, 'ffn_conv_w': ffn_conv_w, 'ffn_conv_b': ffn_conv_b, 'ffn_out_w': ffn_out_w,
        'hyb_in_w': hyb_in_w, 'hyb_out_w': hyb_out_w,
        'gdn_conv_w': gdn_conv_w, 'gdn_a_log': gdn_a_log, 'gdn_dt_bias': gdn_dt_bias, 'gdn_norm': gdn_norm,
        'lru_conv_w': lru_conv_w, 'lru_conv_b': lru_conv_b, 'lru_wr': lru_wr, 'lru_br': lru_br,
        'lru_wi': lru_wi, 'lru_bi': lru_bi, 'lru_lambda': lru_lambda,
        'swa_in_w': swa_in_w, 'swa_out_w': swa_out_w, 'swa_q_norm': swa_q_norm, 'swa_k_norm': swa_k_norm,
        'swa_sinks': swa_sinks,
    }


def _fwd_reference(x, mem, positions, norm_mix, norm_cross, norm_mem, norm_ffn,
              xq_w, xkv_w, xo_w, xq_norm, xk_norm,
              ffn_in_w, ffn_conv_w, ffn_conv_b, ffn_out_w,
              hyb_in_w, hyb_out_w, gdn_conv_w, gdn_a_log, gdn_dt_bias, gdn_norm,
              lru_conv_w, lru_conv_b, lru_wr, lru_br, lru_wi, lru_bi, lru_lambda,
              swa_in_w, swa_out_w, swa_q_norm, swa_k_norm, swa_sinks):
    cos, sin = rope_tables(positions, SWA_HD)
    for l in range(DEPTH):
        h = rmsnorm(x, norm_mix[l])
        if l % 2 == 0:
            e = l // 2
            qkv, z, b_raw, a_raw, lx, lg = split_cols(h @ hyb_in_w[e], HYB_SIZES)
            oa = gdn_mixer(qkv, z, b_raw, a_raw, gdn_conv_w[e], gdn_a_log[e], gdn_dt_bias[e], gdn_norm[e])
            ob = rglru_mixer(lx, lg, lru_conv_w[e], lru_conv_b[e], lru_wr[e], lru_br[e],
                             lru_wi[e], lru_bi[e], lru_lambda[e])
            mix = jnp.concatenate([oa, ob], axis=-1) @ hyb_out_w[e]
        else:
            o = l // 2
            mix = swa_mixer(h @ swa_in_w[o], swa_q_norm[o], swa_k_norm[o], swa_sinks[o], cos, sin) @ swa_out_w[o]
        x = x + mix
        x = x + cross_attn(rmsnorm(x, norm_cross[l]), rmsnorm(mem, norm_mem[l]),
                           xq_w[l], xkv_w[l], xo_w[l], xq_norm[l], xk_norm[l])
        x = x + conv_ffn(rmsnorm(x, norm_ffn[l]), ffn_in_w[l], ffn_conv_w[l], ffn_conv_b[l], ffn_out_w[l])
    return x


import jax as _jax
import jax.numpy as _jnp

TWIN_FORMAT = 'train_step'
FWD_PARAMS = ['x', 'mem', 'positions', 'norm_mix', 'norm_cross', 'norm_mem', 'norm_ffn', 'xq_w', 'xkv_w', 'xo_w', 'xq_norm', 'xk_norm', 'ffn_in_w', 'ffn_conv_w', 'ffn_conv_b', 'ffn_out_w', 'hyb_in_w', 'hyb_out_w', 'gdn_conv_w', 'gdn_a_log', 'gdn_dt_bias', 'gdn_norm', 'lru_conv_w', 'lru_conv_b', 'lru_wr', 'lru_br', 'lru_wi', 'lru_bi', 'lru_lambda', 'swa_in_w', 'swa_out_w', 'swa_q_norm', 'swa_k_norm', 'swa_sinks']
TWIN_WEIGHTS = ['norm_mix', 'norm_cross', 'norm_mem', 'norm_ffn', 'xq_w', 'xkv_w', 'xo_w', 'xq_norm', 'xk_norm', 'ffn_in_w', 'ffn_conv_w', 'ffn_conv_b', 'ffn_out_w', 'hyb_in_w', 'hyb_out_w', 'gdn_conv_w', 'gdn_a_log', 'gdn_dt_bias', 'gdn_norm', 'lru_conv_w', 'lru_conv_b', 'lru_wr', 'lru_br', 'lru_wi', 'lru_bi', 'lru_lambda', 'swa_in_w', 'swa_out_w', 'swa_q_norm', 'swa_k_norm', 'swa_sinks']
TWIN_DIFF_INPUT = 'x'
TWIN_INPUTS = ['x', 'mem', 'positions', 'norm_mix', 'norm_cross', 'norm_mem', 'norm_ffn', 'xq_w', 'xkv_w', 'xo_w', 'xq_norm', 'xk_norm', 'ffn_in_w', 'ffn_conv_w', 'ffn_conv_b', 'ffn_out_w', 'hyb_in_w', 'hyb_out_w', 'gdn_conv_w', 'gdn_a_log', 'gdn_dt_bias', 'gdn_norm', 'lru_conv_w', 'lru_conv_b', 'lru_wr', 'lru_br', 'lru_wi', 'lru_bi', 'lru_lambda', 'swa_in_w', 'swa_out_w', 'swa_q_norm', 'swa_k_norm', 'swa_sinks', 'loss_target', 'm_norm_mix', 'm_norm_cross', 'm_norm_mem', 'm_norm_ffn', 'm_xq_w', 'm_xkv_w', 'm_xo_w', 'm_xq_norm', 'm_xk_norm', 'm_ffn_in_w', 'm_ffn_conv_w', 'm_ffn_conv_b', 'm_ffn_out_w', 'm_hyb_in_w', 'm_hyb_out_w', 'm_gdn_conv_w', 'm_gdn_a_log', 'm_gdn_dt_bias', 'm_gdn_norm', 'm_lru_conv_w', 'm_lru_conv_b', 'm_lru_wr', 'm_lru_br', 'm_lru_wi', 'm_lru_bi', 'm_lru_lambda', 'm_swa_in_w', 'm_swa_out_w', 'm_swa_q_norm', 'm_swa_k_norm', 'm_swa_sinks', 'v_norm_mix', 'v_norm_cross', 'v_norm_mem', 'v_norm_ffn', 'v_xq_w', 'v_xkv_w', 'v_xo_w', 'v_xq_norm', 'v_xk_norm', 'v_ffn_in_w', 'v_ffn_conv_w', 'v_ffn_conv_b', 'v_ffn_out_w', 'v_hyb_in_w', 'v_hyb_out_w', 'v_gdn_conv_w', 'v_gdn_a_log', 'v_gdn_dt_bias', 'v_gdn_norm', 'v_lru_conv_w', 'v_lru_conv_b', 'v_lru_wr', 'v_lru_br', 'v_lru_wi', 'v_lru_bi', 'v_lru_lambda', 'v_swa_in_w', 'v_swa_out_w', 'v_swa_q_norm', 'v_swa_k_norm', 'v_swa_sinks']
TWIN_OUTPUTS = ['loss', 'grad_x', 'grad_norm_mix', 'grad_norm_cross', 'grad_norm_mem', 'grad_norm_ffn', 'grad_xq_w', 'grad_xkv_w', 'grad_xo_w', 'grad_xq_norm', 'grad_xk_norm', 'grad_ffn_in_w', 'grad_ffn_conv_w', 'grad_ffn_conv_b', 'grad_ffn_out_w', 'grad_hyb_in_w', 'grad_hyb_out_w', 'grad_gdn_conv_w', 'grad_gdn_a_log', 'grad_gdn_dt_bias', 'grad_gdn_norm', 'grad_lru_conv_w', 'grad_lru_conv_b', 'grad_lru_wr', 'grad_lru_br', 'grad_lru_wi', 'grad_lru_bi', 'grad_lru_lambda', 'grad_swa_in_w', 'grad_swa_out_w', 'grad_swa_q_norm', 'grad_swa_k_norm', 'grad_swa_sinks', 'delta_norm_mix', 'delta_norm_cross', 'delta_norm_mem', 'delta_norm_ffn', 'delta_xq_w', 'delta_xkv_w', 'delta_xo_w', 'delta_xq_norm', 'delta_xk_norm', 'delta_ffn_in_w', 'delta_ffn_conv_w', 'delta_ffn_conv_b', 'delta_ffn_out_w', 'delta_hyb_in_w', 'delta_hyb_out_w', 'delta_gdn_conv_w', 'delta_gdn_a_log', 'delta_gdn_dt_bias', 'delta_gdn_norm', 'delta_lru_conv_w', 'delta_lru_conv_b', 'delta_lru_wr', 'delta_lru_br', 'delta_lru_wi', 'delta_lru_bi', 'delta_lru_lambda', 'delta_swa_in_w', 'delta_swa_out_w', 'delta_swa_q_norm', 'delta_swa_k_norm', 'delta_swa_sinks', 'new_m_norm_mix', 'new_m_norm_cross', 'new_m_norm_mem', 'new_m_norm_ffn', 'new_m_xq_w', 'new_m_xkv_w', 'new_m_xo_w', 'new_m_xq_norm', 'new_m_xk_norm', 'new_m_ffn_in_w', 'new_m_ffn_conv_w', 'new_m_ffn_conv_b', 'new_m_ffn_out_w', 'new_m_hyb_in_w', 'new_m_hyb_out_w', 'new_m_gdn_conv_w', 'new_m_gdn_a_log', 'new_m_gdn_dt_bias', 'new_m_gdn_norm', 'new_m_lru_conv_w', 'new_m_lru_conv_b', 'new_m_lru_wr', 'new_m_lru_br', 'new_m_lru_wi', 'new_m_lru_bi', 'new_m_lru_lambda', 'new_m_swa_in_w', 'new_m_swa_out_w', 'new_m_swa_q_norm', 'new_m_swa_k_norm', 'new_m_swa_sinks', 'new_v_norm_mix', 'new_v_norm_cross', 'new_v_norm_mem', 'new_v_norm_ffn', 'new_v_xq_w', 'new_v_xkv_w', 'new_v_xo_w', 'new_v_xq_norm', 'new_v_xk_norm', 'new_v_ffn_in_w', 'new_v_ffn_conv_w', 'new_v_ffn_conv_b', 'new_v_ffn_out_w', 'new_v_hyb_in_w', 'new_v_hyb_out_w', 'new_v_gdn_conv_w', 'new_v_gdn_a_log', 'new_v_gdn_dt_bias', 'new_v_gdn_norm', 'new_v_lru_conv_w', 'new_v_lru_conv_b', 'new_v_lru_wr', 'new_v_lru_br', 'new_v_lru_wi', 'new_v_lru_bi', 'new_v_lru_lambda', 'new_v_swa_in_w', 'new_v_swa_out_w', 'new_v_swa_q_norm', 'new_v_swa_k_norm', 'new_v_swa_sinks']
TWIN_LEAF_KINDS = {'loss': 'loss', 'grad_x': 'grad_x', 'grad_norm_mix': 'grad_w', 'grad_norm_cross': 'grad_w', 'grad_norm_mem': 'grad_w', 'grad_norm_ffn': 'grad_w', 'grad_xq_w': 'grad_w', 'grad_xkv_w': 'grad_w', 'grad_xo_w': 'grad_w', 'grad_xq_norm': 'grad_w', 'grad_xk_norm': 'grad_w', 'grad_ffn_in_w': 'grad_w', 'grad_ffn_conv_w': 'grad_w', 'grad_ffn_conv_b': 'grad_w', 'grad_ffn_out_w': 'grad_w', 'grad_hyb_in_w': 'grad_w', 'grad_hyb_out_w': 'grad_w', 'grad_gdn_conv_w': 'grad_w', 'grad_gdn_a_log': 'grad_w', 'grad_gdn_dt_bias': 'grad_w', 'grad_gdn_norm': 'grad_w', 'grad_lru_conv_w': 'grad_w', 'grad_lru_conv_b': 'grad_w', 'grad_lru_wr': 'grad_w', 'grad_lru_br': 'grad_w', 'grad_lru_wi': 'grad_w', 'grad_lru_bi': 'grad_w', 'grad_lru_lambda': 'grad_w', 'grad_swa_in_w': 'grad_w', 'grad_swa_out_w': 'grad_w', 'grad_swa_q_norm': 'grad_w', 'grad_swa_k_norm': 'grad_w', 'grad_swa_sinks': 'grad_w', 'delta_norm_mix': 'delta_w', 'delta_norm_cross': 'delta_w', 'delta_norm_mem': 'delta_w', 'delta_norm_ffn': 'delta_w', 'delta_xq_w': 'delta_w', 'delta_xkv_w': 'delta_w', 'delta_xo_w': 'delta_w', 'delta_xq_norm': 'delta_w', 'delta_xk_norm': 'delta_w', 'delta_ffn_in_w': 'delta_w', 'delta_ffn_conv_w': 'delta_w', 'delta_ffn_conv_b': 'delta_w', 'delta_ffn_out_w': 'delta_w', 'delta_hyb_in_w': 'delta_w', 'delta_hyb_out_w': 'delta_w', 'delta_gdn_conv_w': 'delta_w', 'delta_gdn_a_log': 'delta_w', 'delta_gdn_dt_bias': 'delta_w', 'delta_gdn_norm': 'delta_w', 'delta_lru_conv_w': 'delta_w', 'delta_lru_conv_b': 'delta_w', 'delta_lru_wr': 'delta_w', 'delta_lru_br': 'delta_w', 'delta_lru_wi': 'delta_w', 'delta_lru_bi': 'delta_w', 'delta_lru_lambda': 'delta_w', 'delta_swa_in_w': 'delta_w', 'delta_swa_out_w': 'delta_w', 'delta_swa_q_norm': 'delta_w', 'delta_swa_k_norm': 'delta_w', 'delta_swa_sinks': 'delta_w', 'new_m_norm_mix': 'new_m', 'new_m_norm_cross': 'new_m', 'new_m_norm_mem': 'new_m', 'new_m_norm_ffn': 'new_m', 'new_m_xq_w': 'new_m', 'new_m_xkv_w': 'new_m', 'new_m_xo_w': 'new_m', 'new_m_xq_norm': 'new_m', 'new_m_xk_norm': 'new_m', 'new_m_ffn_in_w': 'new_m', 'new_m_ffn_conv_w': 'new_m', 'new_m_ffn_conv_b': 'new_m', 'new_m_ffn_out_w': 'new_m', 'new_m_hyb_in_w': 'new_m', 'new_m_hyb_out_w': 'new_m', 'new_m_gdn_conv_w': 'new_m', 'new_m_gdn_a_log': 'new_m', 'new_m_gdn_dt_bias': 'new_m', 'new_m_gdn_norm': 'new_m', 'new_m_lru_conv_w': 'new_m', 'new_m_lru_conv_b': 'new_m', 'new_m_lru_wr': 'new_m', 'new_m_lru_br': 'new_m', 'new_m_lru_wi': 'new_m', 'new_m_lru_bi': 'new_m', 'new_m_lru_lambda': 'new_m', 'new_m_swa_in_w': 'new_m', 'new_m_swa_out_w': 'new_m', 'new_m_swa_q_norm': 'new_m', 'new_m_swa_k_norm': 'new_m', 'new_m_swa_sinks': 'new_m', 'new_v_norm_mix': 'new_v', 'new_v_norm_cross': 'new_v', 'new_v_norm_mem': 'new_v', 'new_v_norm_ffn': 'new_v', 'new_v_xq_w': 'new_v', 'new_v_xkv_w': 'new_v', 'new_v_xo_w': 'new_v', 'new_v_xq_norm': 'new_v', 'new_v_xk_norm': 'new_v', 'new_v_ffn_in_w': 'new_v', 'new_v_ffn_conv_w': 'new_v', 'new_v_ffn_conv_b': 'new_v', 'new_v_ffn_out_w': 'new_v', 'new_v_hyb_in_w': 'new_v', 'new_v_hyb_out_w': 'new_v', 'new_v_gdn_conv_w': 'new_v', 'new_v_gdn_a_log': 'new_v', 'new_v_gdn_dt_bias': 'new_v', 'new_v_gdn_norm': 'new_v', 'new_v_lru_conv_w': 'new_v', 'new_v_lru_conv_b': 'new_v', 'new_v_lru_wr': 'new_v', 'new_v_lru_br': 'new_v', 'new_v_lru_wi': 'new_v', 'new_v_lru_bi': 'new_v', 'new_v_lru_lambda': 'new_v', 'new_v_swa_in_w': 'new_v', 'new_v_swa_out_w': 'new_v', 'new_v_swa_q_norm': 'new_v', 'new_v_swa_k_norm': 'new_v', 'new_v_swa_sinks': 'new_v'}


def _forward(args):
    return _fwd_reference(*[args[k] for k in FWD_PARAMS])


def _output_shape():
    def fwd():
        inp = _fwd_setup_inputs(0)
        return _fwd_reference(*[inp[k] for k in FWD_PARAMS])
    out = _jax.eval_shape(fwd)
    return out.shape, out.dtype

N_MICROBATCH = 1
ADAM_LR = 0.001
ADAM_B1 = 0.9
ADAM_B2 = 0.999
ADAM_EPS = 1e-08
ADAM_WD = 0.01
ADAM_STEP = 10
PER_EXAMPLE_BATCH_AXIS = {'x': 0, 'mem': 0, 'positions': 0, 'loss_target': 0}
SHARED_INPUTS = []
_WEIGHT_DTYPES = {'norm_mix': _jnp.float32, 'norm_cross': _jnp.float32, 'norm_mem': _jnp.float32, 'norm_ffn': _jnp.float32, 'xq_w': _jnp.float32, 'xkv_w': _jnp.float32, 'xo_w': _jnp.float32, 'xq_norm': _jnp.float32, 'xk_norm': _jnp.float32, 'ffn_in_w': _jnp.float32, 'ffn_conv_w': _jnp.float32, 'ffn_conv_b': _jnp.float32, 'ffn_out_w': _jnp.float32, 'hyb_in_w': _jnp.float32, 'hyb_out_w': _jnp.float32, 'gdn_conv_w': _jnp.float32, 'gdn_a_log': _jnp.float32, 'gdn_dt_bias': _jnp.float32, 'gdn_norm': _jnp.float32, 'lru_conv_w': _jnp.float32, 'lru_conv_b': _jnp.float32, 'lru_wr': _jnp.float32, 'lru_br': _jnp.float32, 'lru_wi': _jnp.float32, 'lru_bi': _jnp.float32, 'lru_lambda': _jnp.float32, 'swa_in_w': _jnp.float32, 'swa_out_w': _jnp.float32, 'swa_q_norm': _jnp.float32, 'swa_k_norm': _jnp.float32, 'swa_sinks': _jnp.float32}
MOMENT_SCALE = {'norm_mix': 1.184140e+01, 'norm_cross': 1.831938e-01, 'norm_mem': 1.271964e+00, 'norm_ffn': 2.500659e+01, 'xq_w': 3.541211e-01, 'xkv_w': 1.557185e+00, 'xo_w': 1.005832e+00, 'xq_norm': 4.801068e+00, 'xk_norm': 4.792853e+00, 'ffn_in_w': 6.512675e-01, 'ffn_conv_w': 2.916841e+00, 'ffn_conv_b': 3.777435e+00, 'ffn_out_w': 1.292404e+00, 'hyb_in_w': 2.654854e+00, 'hyb_out_w': 3.727043e+00, 'gdn_conv_w': 2.685464e+00, 'gdn_a_log': 6.080715e+01, 'gdn_dt_bias': 5.842176e+01, 'gdn_norm': 9.766034e+01, 'lru_conv_w': 7.777823e+00, 'lru_conv_b': 3.998476e+01, 'lru_wr': 1.210583e+00, 'lru_br': 1.079605e+00, 'lru_wi': 2.315964e+00, 'lru_bi': 4.661793e+00, 'lru_lambda': 2.321194e+00, 'swa_in_w': 2.138093e+00, 'swa_out_w': 2.178791e+00, 'swa_q_norm': 1.035327e+01, 'swa_k_norm': 1.037099e+01, 'swa_sinks': 1.042323e+00}


def _to_microbatches(a, axis):
    t = _jnp.moveaxis(a, axis, 0)
    t = t.reshape((N_MICROBATCH, t.shape[0] // N_MICROBATCH) + t.shape[1:])
    return _jnp.moveaxis(t, 1, axis + 1)


def setup_inputs(seed: int = 0) -> dict:
    inp = _fwd_setup_inputs(seed)
    key = _jax.random.fold_in(_jax.random.key(seed), 7919)
    shape, _ = _output_shape()
    out = dict(inp)
    out["loss_target"] = _jax.random.normal(_jax.random.fold_in(key, 0), shape, _jnp.float32)
    for i, name in enumerate(TWIN_WEIGHTS):
        w = inp[name].astype(_jnp.float32)
        if MOMENT_SCALE is None:
            s = _jnp.sqrt(_jnp.mean(_jnp.square(w)) + 1e-30)
        else:
            s = MOMENT_SCALE[name]
        km, kv = _jax.random.split(_jax.random.fold_in(key, i + 1))
        out[name] = w
        out["m_" + name] = s * _jax.random.normal(km, w.shape, _jnp.float32)
        out["v_" + name] = (s * s) * _jax.random.uniform(kv, w.shape, _jnp.float32, 0.5, 1.5)
    if N_MICROBATCH > 1:
        for name, axis in PER_EXAMPLE_BATCH_AXIS.items():
            out[name] = _to_microbatches(out[name], axis)
    return {'x': out['x'], 'mem': out['mem'], 'positions': out['positions'], 'norm_mix': out['norm_mix'], 'norm_cross': out['norm_cross'], 'norm_mem': out['norm_mem'], 'norm_ffn': out['norm_ffn'], 'xq_w': out['xq_w'], 'xkv_w': out['xkv_w'], 'xo_w': out['xo_w'], 'xq_norm': out['xq_norm'], 'xk_norm': out['xk_norm'], 'ffn_in_w': out['ffn_in_w'], 'ffn_conv_w': out['ffn_conv_w'], 'ffn_conv_b': out['ffn_conv_b'], 'ffn_out_w': out['ffn_out_w'], 'hyb_in_w': out['hyb_in_w'], 'hyb_out_w': out['hyb_out_w'], 'gdn_conv_w': out['gdn_conv_w'], 'gdn_a_log': out['gdn_a_log'], 'gdn_dt_bias': out['gdn_dt_bias'], 'gdn_norm': out['gdn_norm'], 'lru_conv_w': out['lru_conv_w'], 'lru_conv_b': out['lru_conv_b'], 'lru_wr': out['lru_wr'], 'lru_br': out['lru_br'], 'lru_wi': out['lru_wi'], 'lru_bi': out['lru_bi'], 'lru_lambda': out['lru_lambda'], 'swa_in_w': out['swa_in_w'], 'swa_out_w': out['swa_out_w'], 'swa_q_norm': out['swa_q_norm'], 'swa_k_norm': out['swa_k_norm'], 'swa_sinks': out['swa_sinks'], 'loss_target': out['loss_target'], 'm_norm_mix': out['m_norm_mix'], 'm_norm_cross': out['m_norm_cross'], 'm_norm_mem': out['m_norm_mem'], 'm_norm_ffn': out['m_norm_ffn'], 'm_xq_w': out['m_xq_w'], 'm_xkv_w': out['m_xkv_w'], 'm_xo_w': out['m_xo_w'], 'm_xq_norm': out['m_xq_norm'], 'm_xk_norm': out['m_xk_norm'], 'm_ffn_in_w': out['m_ffn_in_w'], 'm_ffn_conv_w': out['m_ffn_conv_w'], 'm_ffn_conv_b': out['m_ffn_conv_b'], 'm_ffn_out_w': out['m_ffn_out_w'], 'm_hyb_in_w': out['m_hyb_in_w'], 'm_hyb_out_w': out['m_hyb_out_w'], 'm_gdn_conv_w': out['m_gdn_conv_w'], 'm_gdn_a_log': out['m_gdn_a_log'], 'm_gdn_dt_bias': out['m_gdn_dt_bias'], 'm_gdn_norm': out['m_gdn_norm'], 'm_lru_conv_w': out['m_lru_conv_w'], 'm_lru_conv_b': out['m_lru_conv_b'], 'm_lru_wr': out['m_lru_wr'], 'm_lru_br': out['m_lru_br'], 'm_lru_wi': out['m_lru_wi'], 'm_lru_bi': out['m_lru_bi'], 'm_lru_lambda': out['m_lru_lambda'], 'm_swa_in_w': out['m_swa_in_w'], 'm_swa_out_w': out['m_swa_out_w'], 'm_swa_q_norm': out['m_swa_q_norm'], 'm_swa_k_norm': out['m_swa_k_norm'], 'm_swa_sinks': out['m_swa_sinks'], 'v_norm_mix': out['v_norm_mix'], 'v_norm_cross': out['v_norm_cross'], 'v_norm_mem': out['v_norm_mem'], 'v_norm_ffn': out['v_norm_ffn'], 'v_xq_w': out['v_xq_w'], 'v_xkv_w': out['v_xkv_w'], 'v_xo_w': out['v_xo_w'], 'v_xq_norm': out['v_xq_norm'], 'v_xk_norm': out['v_xk_norm'], 'v_ffn_in_w': out['v_ffn_in_w'], 'v_ffn_conv_w': out['v_ffn_conv_w'], 'v_ffn_conv_b': out['v_ffn_conv_b'], 'v_ffn_out_w': out['v_ffn_out_w'], 'v_hyb_in_w': out['v_hyb_in_w'], 'v_hyb_out_w': out['v_hyb_out_w'], 'v_gdn_conv_w': out['v_gdn_conv_w'], 'v_gdn_a_log': out['v_gdn_a_log'], 'v_gdn_dt_bias': out['v_gdn_dt_bias'], 'v_gdn_norm': out['v_gdn_norm'], 'v_lru_conv_w': out['v_lru_conv_w'], 'v_lru_conv_b': out['v_lru_conv_b'], 'v_lru_wr': out['v_lru_wr'], 'v_lru_br': out['v_lru_br'], 'v_lru_wi': out['v_lru_wi'], 'v_lru_bi': out['v_lru_bi'], 'v_lru_lambda': out['v_lru_lambda'], 'v_swa_in_w': out['v_swa_in_w'], 'v_swa_out_w': out['v_swa_out_w'], 'v_swa_q_norm': out['v_swa_q_norm'], 'v_swa_k_norm': out['v_swa_k_norm'], 'v_swa_sinks': out['v_swa_sinks']}


def _loss(weights, diff, rest, loss_target):
    with _jax.named_scope("forward"):
        args = {**rest, TWIN_DIFF_INPUT: diff, **{k: w.astype(_WEIGHT_DTYPES[k]) for k, w in weights.items()}}
        y = _forward(args)
    with _jax.named_scope("loss_head"):
        err = _jnp.square(y.astype(_jnp.float32) - loss_target)
        return 0.5 * _jnp.sum(_jnp.mean(err, axis=-1)) if err.ndim else 0.5 * err


def _adamw(w, g, m, v):
    m = ADAM_B1 * m + (1.0 - ADAM_B1) * g
    v = ADAM_B2 * v + (1.0 - ADAM_B2) * _jnp.square(g)
    m_hat = m / (1.0 - ADAM_B1 ** ADAM_STEP)
    v_hat = v / (1.0 - ADAM_B2 ** ADAM_STEP)
    delta = -ADAM_LR * (m_hat / (_jnp.sqrt(v_hat) + ADAM_EPS) + ADAM_WD * w)
    return delta, m, v


def reference(x, mem, positions, norm_mix, norm_cross, norm_mem, norm_ffn, xq_w, xkv_w, xo_w, xq_norm, xk_norm, ffn_in_w, ffn_conv_w, ffn_conv_b, ffn_out_w, hyb_in_w, hyb_out_w, gdn_conv_w, gdn_a_log, gdn_dt_bias, gdn_norm, lru_conv_w, lru_conv_b, lru_wr, lru_br, lru_wi, lru_bi, lru_lambda, swa_in_w, swa_out_w, swa_q_norm, swa_k_norm, swa_sinks, loss_target, m_norm_mix, m_norm_cross, m_norm_mem, m_norm_ffn, m_xq_w, m_xkv_w, m_xo_w, m_xq_norm, m_xk_norm, m_ffn_in_w, m_ffn_conv_w, m_ffn_conv_b, m_ffn_out_w, m_hyb_in_w, m_hyb_out_w, m_gdn_conv_w, m_gdn_a_log, m_gdn_dt_bias, m_gdn_norm, m_lru_conv_w, m_lru_conv_b, m_lru_wr, m_lru_br, m_lru_wi, m_lru_bi, m_lru_lambda, m_swa_in_w, m_swa_out_w, m_swa_q_norm, m_swa_k_norm, m_swa_sinks, v_norm_mix, v_norm_cross, v_norm_mem, v_norm_ffn, v_xq_w, v_xkv_w, v_xo_w, v_xq_norm, v_xk_norm, v_ffn_in_w, v_ffn_conv_w, v_ffn_conv_b, v_ffn_out_w, v_hyb_in_w, v_hyb_out_w, v_gdn_conv_w, v_gdn_a_log, v_gdn_dt_bias, v_gdn_norm, v_lru_conv_w, v_lru_conv_b, v_lru_wr, v_lru_br, v_lru_wi, v_lru_bi, v_lru_lambda, v_swa_in_w, v_swa_out_w, v_swa_q_norm, v_swa_k_norm, v_swa_sinks):
    given = dict(x=x, mem=mem, positions=positions, norm_mix=norm_mix, norm_cross=norm_cross, norm_mem=norm_mem, norm_ffn=norm_ffn, xq_w=xq_w, xkv_w=xkv_w, xo_w=xo_w, xq_norm=xq_norm, xk_norm=xk_norm, ffn_in_w=ffn_in_w, ffn_conv_w=ffn_conv_w, ffn_conv_b=ffn_conv_b, ffn_out_w=ffn_out_w, hyb_in_w=hyb_in_w, hyb_out_w=hyb_out_w, gdn_conv_w=gdn_conv_w, gdn_a_log=gdn_a_log, gdn_dt_bias=gdn_dt_bias, gdn_norm=gdn_norm, lru_conv_w=lru_conv_w, lru_conv_b=lru_conv_b, lru_wr=lru_wr, lru_br=lru_br, lru_wi=lru_wi, lru_bi=lru_bi, lru_lambda=lru_lambda, swa_in_w=swa_in_w, swa_out_w=swa_out_w, swa_q_norm=swa_q_norm, swa_k_norm=swa_k_norm, swa_sinks=swa_sinks, loss_target=loss_target, m_norm_mix=m_norm_mix, m_norm_cross=m_norm_cross, m_norm_mem=m_norm_mem, m_norm_ffn=m_norm_ffn, m_xq_w=m_xq_w, m_xkv_w=m_xkv_w, m_xo_w=m_xo_w, m_xq_norm=m_xq_norm, m_xk_norm=m_xk_norm, m_ffn_in_w=m_ffn_in_w, m_ffn_conv_w=m_ffn_conv_w, m_ffn_conv_b=m_ffn_conv_b, m_ffn_out_w=m_ffn_out_w, m_hyb_in_w=m_hyb_in_w, m_hyb_out_w=m_hyb_out_w, m_gdn_conv_w=m_gdn_conv_w, m_gdn_a_log=m_gdn_a_log, m_gdn_dt_bias=m_gdn_dt_bias, m_gdn_norm=m_gdn_norm, m_lru_conv_w=m_lru_conv_w, m_lru_conv_b=m_lru_conv_b, m_lru_wr=m_lru_wr, m_lru_br=m_lru_br, m_lru_wi=m_lru_wi, m_lru_bi=m_lru_bi, m_lru_lambda=m_lru_lambda, m_swa_in_w=m_swa_in_w, m_swa_out_w=m_swa_out_w, m_swa_q_norm=m_swa_q_norm, m_swa_k_norm=m_swa_k_norm, m_swa_sinks=m_swa_sinks, v_norm_mix=v_norm_mix, v_norm_cross=v_norm_cross, v_norm_mem=v_norm_mem, v_norm_ffn=v_norm_ffn, v_xq_w=v_xq_w, v_xkv_w=v_xkv_w, v_xo_w=v_xo_w, v_xq_norm=v_xq_norm, v_xk_norm=v_xk_norm, v_ffn_in_w=v_ffn_in_w, v_ffn_conv_w=v_ffn_conv_w, v_ffn_conv_b=v_ffn_conv_b, v_ffn_out_w=v_ffn_out_w, v_hyb_in_w=v_hyb_in_w, v_hyb_out_w=v_hyb_out_w, v_gdn_conv_w=v_gdn_conv_w, v_gdn_a_log=v_gdn_a_log, v_gdn_dt_bias=v_gdn_dt_bias, v_gdn_norm=v_gdn_norm, v_lru_conv_w=v_lru_conv_w, v_lru_conv_b=v_lru_conv_b, v_lru_wr=v_lru_wr, v_lru_br=v_lru_br, v_lru_wi=v_lru_wi, v_lru_bi=v_lru_bi, v_lru_lambda=v_lru_lambda, v_swa_in_w=v_swa_in_w, v_swa_out_w=v_swa_out_w, v_swa_q_norm=v_swa_q_norm, v_swa_k_norm=v_swa_k_norm, v_swa_sinks=v_swa_sinks)
    weights = {n: given[n] for n in TWIN_WEIGHTS}
    shared = {n: given[n] for n in SHARED_INPUTS}
    per_example = {n: given[n] for n in ['x', 'mem', 'positions']}
    grad_fn = _jax.value_and_grad(_loss, argnums=(0, 1))

    def one_microbatch(ex, loss_target):
        ex = dict(ex)
        diff = ex.pop(TWIN_DIFF_INPUT)
        return grad_fn(weights, diff, {**shared, **ex}, loss_target)

    if N_MICROBATCH == 1:
        loss, (grad_w, grad_x) = one_microbatch(per_example, given["loss_target"])
    else:
        def body(carry, xs):
            loss_sum, grad_sum = carry
            l_k, (gw_k, gx_k) = one_microbatch(xs[0], xs[1])
            with _jax.named_scope("update"):
                return (loss_sum + l_k, _jax.tree.map(_jnp.add, grad_sum, gw_k)), gx_k

        init = (_jnp.zeros((), _jnp.float32), _jax.tree.map(_jnp.zeros_like, weights))
        (loss, grad_w), grad_x = _jax.lax.scan(body, init, (per_example, given["loss_target"]))
    with _jax.named_scope("update"):
        delta_w, new_m, new_v = {}, {}, {}
        for n in TWIN_WEIGHTS:
            delta_w[n], new_m[n], new_v[n] = _adamw(weights[n], grad_w[n], given["m_" + n], given["v_" + n])
    return (loss, grad_x, *[grad_w[n] for n in TWIN_WEIGHTS], *[delta_w[n] for n in TWIN_WEIGHTS],
            *[new_m[n] for n in TWIN_WEIGHTS], *[new_v[n] for n in TWIN_WEIGHTS])
```

```python
import functools
import math

import jax
import jax.numpy as jnp
import numpy as np
from jax import lax
from jax.experimental import pallas as pl
from jax.experimental.pallas import tpu as pltpu

F32 = jnp.float32
BF16 = jnp.bfloat16
HIGHEST = lax.Precision.HIGHEST

EPS = 1e-6
N_DEV = 8
GDN_HEADS = 8
GDN_DK = 128
GDN_CHUNK = 64
GDN_QKV = 3072
GDN_V = 1024
LRU_WIDTH = 1024
LRU_BLOCKS = 8
LRU_C = 8.0
SWA_HEADS = 32
SWA_KV_HEADS = 8
SWA_HD = 64
SWA_BLOCK = 128
SWA_Q = SWA_HEADS * SWA_HD
SWA_KV = SWA_KV_HEADS * SWA_HD
ROPE_THETA = 10000.0
X_HEADS = 4
X_HD = 128
X_INNER = X_HEADS * X_HD
HYB_IN = 6160
HYB_PAD = 6272
LANE = 128
SUB = 8
ADAM_LR = 0.001
ADAM_B1 = 0.9
ADAM_B2 = 0.999
ADAM_EPS = 1e-08
ADAM_WD = 0.01
ADAM_STEP = 10
PACK_COLS = 1024
VMEM_LIMIT = 56 * 1024 * 1024


def _params(n_grid):
    return pltpu.CompilerParams(dimension_semantics=("arbitrary",) * n_grid, vmem_limit_bytes=VMEM_LIMIT)


def _dg(a, b, ca, cb, hi):
    dims = (((ca,), (cb,)), ((), ()))
    if hi:
        return lax.dot_general(a, b, dims, precision=HIGHEST, preferred_element_type=F32)
    return lax.dot_general(a.astype(BF16), b.astype(BF16), dims, preferred_element_type=F32)


@functools.partial(jax.custom_vjp, nondiff_argnums=(2, 3, 4))
def mm(a, b, ca, cb, hi):
    return _dg(a, b, ca, cb, hi)


def _mm_fwd(a, b, ca, cb, hi):
    return _dg(a, b, ca, cb, hi), (a, b)


def _mm_bwd(ca, cb, hi, res, g):
    a, b = res
    if ca == 1:
        da = _dg(g, b, 1, 1 - cb, hi)
    else:
        da = _dg(b, g, 1 - cb, 1, hi)
    if cb == 0:
        db = _dg(a, g, 1 - ca, 0, hi)
    else:
        db = _dg(g, a, 0, 1 - ca, hi)
    return da, db


mm.defvjp(_mm_fwd, _mm_bwd)


def mm_nn(a, b, hi=False):
    return mm(a, b, 1, 0, hi)


def mm_nt(a, b, hi=False):
    return mm(a, b, 1, 1, hi)


def mm_tn(a, b, hi=False):
    return mm(a, b, 0, 0, hi)


def _rows(shape):
    return lax.broadcasted_iota(jnp.int32, shape, 0)


def _lanes(shape):
    return lax.broadcasted_iota(jnp.int32, shape, 1)


@functools.partial(jax.custom_vjp, nondiff_argnums=(2,))
def shift_rows(x, tail, k):
    rx = pltpu.roll(x, k, 0)
    rt = pltpu.roll(tail, k, 0)
    first = jnp.where(_rows(tail.shape) < k, rt, rx[:SUB])
    return jnp.concatenate([first, rx[SUB:]], axis=0)


def _shift_rows_fwd(x, tail, k):
    return shift_rows(x, tail, k), None


def _shift_rows_bwd(k, _, g):
    n = g.shape[0]
    dx = jnp.where(_rows(g.shape) < n - k, pltpu.roll(g, n - k, 0), 0.0)
    g8 = g[:SUB]
    dtail = jnp.where(_rows(g8.shape) >= SUB - k, pltpu.roll(g8, SUB - k, 0), 0.0)
    return dx, dtail


shift_rows.defvjp(_shift_rows_fwd, _shift_rows_bwd)


@functools.partial(jax.custom_vjp, nondiff_argnums=(1, 2))
def shift_fill(x, k, fill):
    return jnp.where(_rows(x.shape) >= k, pltpu.roll(x, k, 0), fill)


def _shift_fill_fwd(x, k, fill):
    return shift_fill(x, k, fill), None


def _shift_fill_bwd(k, fill, _, g):
    n = g.shape[0]
    return (jnp.where(_rows(g.shape) < n - k, pltpu.roll(g, n - k, 0), 0.0),)


shift_fill.defvjp(_shift_fill_fwd, _shift_fill_bwd)


@jax.custom_vjp
def inv_unit_lower(l):
    n = l.shape[0]
    eye = (_rows(l.shape) == _lanes(l.shape)).astype(F32)
    t = eye - l
    p = l
    k = 2
    while k < n:
        p = _dg(p, p, 1, 0, True)
        t = _dg(t, eye + p, 1, 0, True)
        k *= 2
    return t


def _inv_fwd(l):
    t = inv_unit_lower(l)
    return t, t


def _inv_bwd(t, g):
    x = _dg(g, t, 1, 1, True)
    return (-_dg(t, x, 0, 0, True),)


inv_unit_lower.defvjp(_inv_fwd, _inv_bwd)


def _row_of(w, j):
    return jnp.sum(jnp.where(_rows(w.shape) == j, w, 0.0), axis=0, keepdims=True)


def _lane_col(x, j):
    return jnp.sum(jnp.where(_lanes(x.shape) == j, x, 0.0), axis=1, keepdims=True)


def _scalar_at(w, r, j):
    m = (_rows(w.shape) == r) & (_lanes(w.shape) == j)
    s = jnp.sum(jnp.where(m, w, 0.0), axis=1, keepdims=True)
    return jnp.sum(s, axis=0, keepdims=True)


def _col2row(col, n):
    eye = _rows((n, n)) == _lanes((n, n))
    return jnp.sum(jnp.where(eye, jnp.broadcast_to(col, (n, n)), 0.0), axis=0, keepdims=True)


def _rms(x, g):
    return x * lax.rsqrt(jnp.mean(x * x, axis=-1, keepdims=True) + EPS) * g


def _conv_core(x, tail, w, width):
    y = x * _row_of(w, width - 1)
    for k in range(1, width):
        y = y + shift_rows(x, tail, k) * _row_of(w, width - 1 - k)
    return y


def _bf16_round(a):
    return a.astype(BF16).astype(F32)


@functools.partial(jax.custom_vjp, nondiff_argnums=(3,))
def _conv(x, tail, w, width):
    return _conv_core(_bf16_round(x), _bf16_round(tail), _bf16_round(w), width)


def _conv_fwd(x, tail, w, width):
    xb, tb, wb = _bf16_round(x), _bf16_round(tail), _bf16_round(w)
    return _conv_core(xb, tb, wb, width), (xb, tb, wb)


def _conv_bwd(width, res, g):
    _, vjp = jax.vjp(functools.partial(_conv_core, width=width), *res)
    return vjp(_bf16_round(g))


_conv.defvjp(_conv_fwd, _conv_bwd)


def _neg_expm1(x):
    t = jnp.tanh(0.5 * x)
    return -2.0 * t / (1.0 - t)


def _gdn_chunk(qr, kr, vr, tq, tk, tv, z, ba, wq, wk, wv, sc, ng, state, head):
    c = GDN_CHUNK
    qc = jax.nn.silu(_conv(qr, tq, wq, 4))
    kc = jax.nn.silu(_conv(kr, tk, wk, 4))
    v = jax.nn.silu(_conv(vr, tv, wv, 4))
    q = qc * lax.rsqrt(jnp.sum(qc * qc, axis=-1, keepdims=True) + EPS) * (GDN_DK ** -0.5)
    k = kc * lax.rsqrt(jnp.sum(kc * kc, axis=-1, keepdims=True) + EPS)
    beta = jax.nn.sigmoid(_lane_col(ba, head))
    a_raw = _lane_col(ba, head + GDN_HEADS)
    a_log = _scalar_at(sc, 0, head)
    dt_bias = _scalar_at(sc, 1, head)
    gl = -jnp.exp(a_log) * jax.nn.softplus(a_raw + dt_bias)
    causal = _rows((c, c)) >= _lanes((c, c))
    strict = _rows((c, c)) > _lanes((c, c))
    gl_row = _col2row(gl, c)
    gc = jnp.sum(jnp.where(causal, gl_row, 0.0), axis=1, keepdims=True)
    gc_row = _col2row(gc, c)
    decay = jnp.exp(jnp.where(causal, gc - gc_row, -jnp.inf))
    kb = k * beta
    l_mat = jnp.where(strict, mm_nt(kb, k) * decay, 0.0)
    tmat = inv_unit_lower(l_mat)
    eg = jnp.exp(gc)
    u = mm_nn(tmat, v * beta)
    w = mm_nn(tmat, kb * eg)
    attn = jnp.where(causal, mm_nt(q, k) * decay, 0.0)
    q_dec = q * eg
    g_last = jnp.sum(gl, axis=0, keepdims=True)
    k_dec = k * jnp.exp(g_last - gc)
    v_new = u - mm_nn(w, state)
    o = mm_nn(q_dec, state) + mm_nn(attn, v_new)
    new_state = state * jnp.exp(g_last) + mm_tn(k_dec, v_new)
    return _rms(o, ng) * jax.nn.silu(z), new_state


def _lru_block(lx, tail, lg, cw, cb, wr, br, wi, bi, lam, hprev):
    n = lx.shape[0]
    xc = _conv(lx, tail, cw, 4) + cb
    r = jax.nn.sigmoid(mm_nn(xc, wr) + br)
    i = jax.nn.sigmoid(mm_nn(xc, wi) + bi)
    log_a = -LRU_C * r * jax.nn.softplus(-lam)
    a = jnp.exp(log_a)
    b = jnp.sqrt(_neg_expm1(2.0 * log_a)) * (i * xc)
    k = 1
    while k < n:
        b = a * shift_fill(b, k, 0.0) + b
        a = a * shift_fill(a, k, 1.0)
        k *= 2
    h = b + a * hprev
    hlast = jnp.sum(jnp.where(_rows(h.shape) == n - 1, h, 0.0), axis=0, keepdims=True)
    return h * jax.nn.gelu(lg), hlast


def _rope(x, cos, sin, rot):
    return x * cos + mm_nn(x, rot, True) * sin


def _swa_kvhead(qs, kc, kp, vc, vp, cosc, sinc, cosp, sinp, qn, kn, sinks, head0, not_first, rot):
    n = kc.shape[0]
    first_off = jnp.where(not_first, 0, n)
    scale = SWA_HD ** -0.5
    kc_r = _rope(_rms(kc, kn), cosc, sinc, rot)
    kp_r = _rope(_rms(kp, kn), cosp, sinp, rot)
    qi = _rows((n, n))
    kj = _lanes((n, n))
    outs = []
    for i, qh in enumerate(qs):
        q_r = _rope(_rms(qh, qn), cosc, sinc, rot)
        s_c = jnp.where(kj <= qi, mm_nt(q_r, kc_r) * scale, -jnp.inf)
        s_p = jnp.where(kj > qi + first_off, mm_nt(q_r, kp_r) * scale, -jnp.inf)
        sink = _scalar_at(sinks, 0, head0 + i)
        m = jnp.maximum(jnp.maximum(jnp.max(s_c, axis=1, keepdims=True), jnp.max(s_p, axis=1, keepdims=True)), sink)
        m = lax.stop_gradient(m)
        pc = jnp.exp(s_c - m)
        pp = jnp.exp(s_p - m)
        denom = jnp.sum(pc, axis=1, keepdims=True) + jnp.sum(pp, axis=1, keepdims=True) + jnp.exp(sink - m)
        outs.append(mm_nn(pc / denom, vc) + mm_nn(pp / denom, vp))
    return tuple(outs)


def _xattn_head(q, k, v, qn, kn):
    qh = _rms(q, qn)
    kh = _rms(k, kn)
    s = mm_nt(qh, kh) * (X_HD ** -0.5)
    m = lax.stop_gradient(jnp.max(s, axis=1, keepdims=True))
    p = jnp.exp(s - m)
    p = p / jnp.sum(p, axis=1, keepdims=True)
    return mm_nn(p, v)


def _ffn_block(gate, tail, up, cw, cb):
    gt = _conv(gate, tail, cw, 3) + cb
    return jax.nn.silu(gt) * up


def _tile(n, cands):
    for c in cands:
        if n % c == 0:
            return c
    return n


def matmul(a, b, mode, res=None, out_dtype=F32, name="mm"):
    if mode == "nn":
        (m, k), (_, n) = a.shape, b.shape
    elif mode == "nt":
        (m, k), (n, _) = a.shape, b.shape
    else:
        (k, m), (_, n) = a.shape, b.shape
    tm = _tile(m, (512, 256, 128))
    tn = _tile(n, (1024, 896, 768, 640, 512, 384, 256, 128))
    tk = _tile(k, (1024, 896, 512, 256, 128))
    nk = k // tk
    ca, cb = {"nn": (1, 0), "nt": (1, 1), "tn": (0, 0)}[mode]
    a_spec = pl.BlockSpec((tk, tm), lambda i, j, kk: (kk, i)) if mode == "tn" else pl.BlockSpec((tm, tk), lambda i, j, kk: (i, kk))
    b_spec = pl.BlockSpec((tn, tk), lambda i, j, kk: (j, kk)) if mode == "nt" else pl.BlockSpec((tk, tn), lambda i, j, kk: (kk, j))
    o_spec = pl.BlockSpec((tm, tn), lambda i, j, kk: (i, j))
    has_res = res is not None

    def body(*refs):
        if has_res:
            a_ref, b_ref, r_ref, o_ref, acc = refs
        else:
            a_ref, b_ref, o_ref, acc = refs
        kk = pl.program_id(2)

        @pl.when(kk == 0)
        def _():
            acc[...] = jnp.zeros_like(acc)

        acc[...] += _dg(a_ref[...], b_ref[...], ca, cb, False)

        @pl.when(kk == nk - 1)
        def _():
            r = acc[...]
            if has_res:
                r = r + r_ref[...].astype(F32)
            o_ref[...] = r.astype(o_ref.dtype)

    ins = [a, b] + ([res] if has_res else [])
    specs = [a_spec, b_spec] + ([o_spec] if has_res else [])
    return pl.pallas_call(
        body, name=name, grid=(m // tm, n // tn, nk), in_specs=specs, out_specs=o_spec,
        out_shape=jax.ShapeDtypeStruct((m, n), out_dtype), scratch_shapes=[pltpu.VMEM((tm, tn), F32)],
        compiler_params=_params(3))(*ins)


def norm_fwd(x, g, name="norm_fwd"):
    t, d = x.shape
    tb = _tile(t, (512, 256, 128, 64))

    def body(x_ref, g_ref, o_ref):
        o_ref[...] = _rms(x_ref[...], g_ref[...]).astype(o_ref.dtype)

    return pl.pallas_call(
        body, name=name, grid=(t // tb,),
        in_specs=[pl.BlockSpec((tb, d), lambda i: (i, 0)), pl.BlockSpec((1, d), lambda i: (0, 0))],
        out_specs=pl.BlockSpec((tb, d), lambda i: (i, 0)), out_shape=jax.ShapeDtypeStruct((t, d), BF16),
        compiler_params=_params(1))(x, g)


def norm_bwd(x, g, dh, dres=None, name="norm_bwd"):
    t, d = x.shape
    tb = _tile(t, (256, 128, 64))
    with_dx = dres is not None

    def body(*refs):
        if with_dx:
            x_ref, g_ref, dh_ref, dr_ref, dx_ref, dg_ref = refs
        else:
            x_ref, g_ref, dh_ref, dg_ref = refs
        _, vjp = jax.vjp(_rms, x_ref[...], g_ref[...])
        dx, dg = vjp(dh_ref[...].astype(F32))

        @pl.when(pl.program_id(0) == 0)
        def _():
            dg_ref[...] = jnp.zeros_like(dg_ref)

        dg_ref[...] += dg
        if with_dx:
            dx_ref[...] = dx + dr_ref[...]

    blk = pl.BlockSpec((tb, d), lambda i: (i, 0))
    gsp = pl.BlockSpec((1, d), lambda i: (0, 0))
    if with_dx:
        return pl.pallas_call(
            body, name=name, grid=(t // tb,), in_specs=[blk, gsp, blk, blk], out_specs=(blk, gsp),
            out_shape=(jax.ShapeDtypeStruct((t, d), F32), jax.ShapeDtypeStruct((1, d), F32)),
            compiler_params=_params(1))(x, g, dh, dres)
    return pl.pallas_call(
        body, name=name, grid=(t // tb,), in_specs=[blk, gsp, blk], out_specs=gsp,
        out_shape=jax.ShapeDtypeStruct((1, d), F32), compiler_params=_params(1))(x, g, dh)


def loss_grad(y, target):
    t, d = y.shape
    tb = _tile(t, (512, 256, 128, 64))

    def body(y_ref, t_ref, dy_ref, l_ref):
        err = y_ref[...] - t_ref[...]
        dy_ref[...] = err / d

        @pl.when(pl.program_id(0) == 0)
        def _():
            l_ref[...] = jnp.zeros_like(l_ref)

        part = jnp.sum(jnp.mean(err * err, axis=-1, keepdims=True), axis=0, keepdims=True)
        l_ref[...] += 0.5 * part

    blk = pl.BlockSpec((tb, d), lambda i: (i, 0))
    lsp = pl.BlockSpec((SUB, LANE), lambda i: (0, 0))
    return pl.pallas_call(
        body, name="loss_grad", grid=(t // tb,), in_specs=[blk, blk], out_specs=(blk, lsp),
        out_shape=(jax.ShapeDtypeStruct((t, d), F32), jax.ShapeDtypeStruct((SUB, LANE), F32)),
        compiler_params=_params(1))(y, target)


def _tail_map(tb, col):
    return lambda g, t: (jnp.maximum(t * (tb // SUB) - 1, 0), col(g))


def _tail_map_rev(tb, nb, col):
    return lambda g, t: (jnp.maximum((nb - 1 - t) * (tb // SUB) - 1, 0), col(g))


def gdn_fwd(p, conv_w, sc, ng):
    t = p.shape[0]
    tb = GDN_CHUNK
    nb = t // tb
    h = GDN_HEADS

    def body(q_ref, k_ref, v_ref, tq_ref, tk_ref, tv_ref, z_ref, ba_ref, wq_ref, wk_ref, wv_ref, sc_ref, ng_ref,
             o_ref, s_ref, state):
        head, ti = pl.program_id(0), pl.program_id(1)

        @pl.when(ti == 0)
        def _():
            state[...] = jnp.zeros_like(state)

        live = (ti > 0).astype(F32)
        s_ref[...] = state[...]
        o, new_state = _gdn_chunk(
            q_ref[...], k_ref[...], v_ref[...], tq_ref[...] * live, tk_ref[...] * live, tv_ref[...] * live,
            z_ref[...], ba_ref[...], wq_ref[...], wk_ref[...], wv_ref[...], sc_ref[...], ng_ref[...], state[...], head)
        o_ref[...] = o.astype(o_ref.dtype)
        state[...] = new_state

    def blk(off):
        return pl.BlockSpec((tb, LANE), lambda g, ti: (ti, off + g))

    def tail(off):
        return pl.BlockSpec((SUB, LANE), _tail_map(tb, lambda g: off + g))

    def wsp(off):
        return pl.BlockSpec((4, LANE), lambda g, ti: (0, off + g))

    in_specs = [blk(0), blk(h), blk(2 * h), tail(0), tail(h), tail(2 * h), blk(3 * h),
                pl.BlockSpec((tb, LANE), lambda g, ti: (ti, 6 * h)), wsp(0), wsp(h), wsp(2 * h),
                pl.BlockSpec((SUB, LANE), lambda g, ti: (0, 0)), pl.BlockSpec((1, LANE), lambda g, ti: (0, 0))]
    out_specs = (pl.BlockSpec((tb, LANE), lambda g, ti: (ti, g)),
                 pl.BlockSpec((None, None, LANE, LANE), lambda g, ti: (g, ti, 0, 0)))
    return pl.pallas_call(
        body, name="gdn_fwd", grid=(h, nb), in_specs=in_specs, out_specs=out_specs,
        out_shape=(jax.ShapeDtypeStruct((t, GDN_V), BF16), jax.ShapeDtypeStruct((h, nb, LANE, LANE), F32)),
        scratch_shapes=[pltpu.VMEM((LANE, LANE), F32)], compiler_params=_params(2))(
            p, p, p, p, p, p, p, p, conv_w, conv_w, conv_w, sc, ng)


def gdn_bwd(p, conv_w, sc, ng, states, dmix):
    t = p.shape[0]
    tb = GDN_CHUNK
    nb = t // tb
    h = GDN_HEADS

    def body(q_ref, k_ref, v_ref, tq_ref, tk_ref, tv_ref, z_ref, ba_ref, wq_ref, wk_ref, wv_ref, sc_ref, ng_ref,
             s_ref, do_ref, dq_ref, dk_ref, dv_ref, dz_ref, dba_ref, dwq_ref, dwk_ref, dwv_ref, dsc_ref, dng_ref,
             dstate, cq, ck, cv):
        head, step = pl.program_id(0), pl.program_id(1)
        ti = nb - 1 - step

        @pl.when(step == 0)
        def _():
            dstate[...] = jnp.zeros_like(dstate)
            cq[...] = jnp.zeros_like(cq)
            ck[...] = jnp.zeros_like(ck)
            cv[...] = jnp.zeros_like(cv)
            dwq_ref[...] = jnp.zeros_like(dwq_ref)
            dwk_ref[...] = jnp.zeros_like(dwk_ref)
            dwv_ref[...] = jnp.zeros_like(dwv_ref)

        @pl.when((step == 0) & (head == 0))
        def _():
            dsc_ref[...] = jnp.zeros_like(dsc_ref)
            dng_ref[...] = jnp.zeros_like(dng_ref)

        live = (ti > 0).astype(F32)
        fn = functools.partial(_gdn_chunk, head=head)
        _, vjp = jax.vjp(
            fn, q_ref[...], k_ref[...], v_ref[...], tq_ref[...] * live, tk_ref[...] * live, tv_ref[...] * live,
            z_ref[...], ba_ref[...], wq_ref[...], wk_ref[...], wv_ref[...], sc_ref[...], ng_ref[...], s_ref[...])
        (dq, dk, dv, dtq, dtk, dtv, dz, dba, dwq, dwk, dwv, dsc, dng, ds) = vjp((do_ref[...], dstate[...]))
        pad = jnp.zeros((tb - SUB, LANE), F32)
        dq_ref[...] = (dq + jnp.concatenate([pad, cq[...]], axis=0)).astype(dq_ref.dtype)
        dk_ref[...] = (dk + jnp.concatenate([pad, ck[...]], axis=0)).astype(dk_ref.dtype)
        dv_ref[...] = (dv + jnp.concatenate([pad, cv[...]], axis=0)).astype(dv_ref.dtype)
        cq[...] = dtq * live
        ck[...] = dtk * live
        cv[...] = dtv * live
        dz_ref[...] = dz.astype(dz_ref.dtype)
        dba_ref[...] = dba
        dwq_ref[...] += dwq
        dwk_ref[...] += dwk
        dwv_ref[...] += dwv
        dsc_ref[...] += dsc
        dng_ref[...] += dng
        dstate[...] = ds

    def blk(off):
        return pl.BlockSpec((tb, LANE), lambda g, s: (nb - 1 - s, off + g))

    def tail(off):
        return pl.BlockSpec((SUB, LANE), _tail_map_rev(tb, nb, lambda g: off + g))

    def wsp(off):
        return pl.BlockSpec((4, LANE), lambda g, s: (0, off + g))

    shared8 = pl.BlockSpec((SUB, LANE), lambda g, s: (0, 0))
    shared1 = pl.BlockSpec((1, LANE), lambda g, s: (0, 0))
    in_specs = [blk(0), blk(h), blk(2 * h), tail(0), tail(h), tail(2 * h), blk(3 * h),
                pl.BlockSpec((tb, LANE), lambda g, s: (nb - 1 - s, 6 * h)), wsp(0), wsp(h), wsp(2 * h), shared8, shared1,
                pl.BlockSpec((None, None, LANE, LANE), lambda g, s: (g, nb - 1 - s, 0, 0)), blk(0)]
    oblk = pl.BlockSpec((tb, LANE), lambda g, s: (nb - 1 - s, g))
    wacc = pl.BlockSpec((4, LANE), lambda g, s: (0, g))
    out_specs = (oblk, oblk, oblk, oblk, pl.BlockSpec((None, tb, LANE), lambda g, s: (g, nb - 1 - s, 0)),
                 wacc, wacc, wacc, shared8, shared1)
    act = jax.ShapeDtypeStruct((t, GDN_V), BF16)
    wsh = jax.ShapeDtypeStruct((4, GDN_V), F32)
    out_shape = (act, act, act, act, jax.ShapeDtypeStruct((h, t, LANE), F32), wsh, wsh, wsh,
                 jax.ShapeDtypeStruct((SUB, LANE), F32), jax.ShapeDtypeStruct((1, LANE), F32))
    scratch = [pltpu.VMEM((LANE, LANE), F32)] + [pltpu.VMEM((SUB, LANE), F32)] * 3
    return pl.pallas_call(
        body, name="gdn_bwd", grid=(h, nb), in_specs=in_specs, out_specs=out_specs, out_shape=out_shape,
        scratch_shapes=scratch, compiler_params=_params(2))(
            p, p, p, p, p, p, p, p, conv_w, conv_w, conv_w, sc, ng, states, dmix)


LRU_TB = 256
LRU_X_OFF = 32
LRU_G_OFF = 40


def lru_fwd(p, cw, cb, wr, br, wi, bi, lam):
    t = p.shape[0]
    tb = min(LRU_TB, t)
    nb = t // tb
    g8 = LRU_BLOCKS

    def body(x_ref, tx_ref, lg_ref, cw_ref, cb_ref, wr_ref, br_ref, wi_ref, bi_ref, lam_ref, o_ref, hs_ref, hstate):
        ti = pl.program_id(1)

        @pl.when(ti == 0)
        def _():
            hstate[...] = jnp.zeros_like(hstate)

        live = (ti > 0).astype(F32)
        hs_ref[...] = hstate[...]
        o, hlast = _lru_block(x_ref[...], tx_ref[...] * live, lg_ref[...], cw_ref[...], cb_ref[...], wr_ref[...],
                              br_ref[...], wi_ref[...], bi_ref[...], lam_ref[...], hstate[0:1, :])
        o_ref[...] = o.astype(o_ref.dtype)
        hstate[...] = jnp.broadcast_to(hlast, hstate.shape)

    vec = pl.BlockSpec((1, LANE), lambda g, ti: (0, g))
    mat = pl.BlockSpec((None, LANE, LANE), lambda g, ti: (g, 0, 0))
    in_specs = [pl.BlockSpec((tb, LANE), lambda g, ti: (ti, LRU_X_OFF + g)),
                pl.BlockSpec((SUB, LANE), _tail_map(tb, lambda g: LRU_X_OFF + g)),
                pl.BlockSpec((tb, LANE), lambda g, ti: (ti, LRU_G_OFF + g)),
                pl.BlockSpec((4, LANE), lambda g, ti: (0, g)), vec, mat, vec, mat, vec, vec]
    out_specs = (pl.BlockSpec((tb, LANE), lambda g, ti: (ti, g)),
                 pl.BlockSpec((None, None, SUB, LANE), lambda g, ti: (g, ti, 0, 0)))
    return pl.pallas_call(
        body, name="lru_fwd", grid=(g8, nb), in_specs=in_specs, out_specs=out_specs,
        out_shape=(jax.ShapeDtypeStruct((t, LRU_WIDTH), BF16), jax.ShapeDtypeStruct((g8, nb, SUB, LANE), F32)),
        scratch_shapes=[pltpu.VMEM((SUB, LANE), F32)], compiler_params=_params(2))(
            p, p, p, cw, cb, wr, br, wi, bi, lam)


def lru_bwd(p, cw, cb, wr, br, wi, bi, lam, hs, dmix):
    t = p.shape[0]
    tb = min(LRU_TB, t)
    nb = t // tb
    g8 = LRU_BLOCKS

    def body(x_ref, tx_ref, lg_ref, cw_ref, cb_ref, wr_ref, br_ref, wi_ref, bi_ref, lam_ref, hs_ref, do_ref,
             dx_ref, dlg_ref, dcw_ref, dcb_ref, dwr_ref, dbr_ref, dwi_ref, dbi_ref, dlam_ref, dh, cx):
        step = pl.program_id(1)
        ti = nb - 1 - step

        @pl.when(step == 0)
        def _():
            dh[...] = jnp.zeros_like(dh)
            cx[...] = jnp.zeros_like(cx)
            for r in (dcw_ref, dcb_ref, dwr_ref, dbr_ref, dwi_ref, dbi_ref, dlam_ref):
                r[...] = jnp.zeros_like(r)

        live = (ti > 0).astype(F32)
        _, vjp = jax.vjp(_lru_block, x_ref[...], tx_ref[...] * live, lg_ref[...], cw_ref[...], cb_ref[...], wr_ref[...],
                         br_ref[...], wi_ref[...], bi_ref[...], lam_ref[...], hs_ref[0:1, :])
        dx, dtx, dlg, dcw, dcb, dwr, dbr, dwi, dbi, dlam, dhp = vjp((do_ref[...], dh[0:1, :]))
        pad = jnp.zeros((tb - SUB, LANE), F32)
        dx_ref[...] = (dx + jnp.concatenate([pad, cx[...]], axis=0)).astype(dx_ref.dtype)
        cx[...] = dtx * live
        dlg_ref[...] = dlg.astype(dlg_ref.dtype)
        dcw_ref[...] += dcw
        dcb_ref[...] += dcb
        dwr_ref[...] += dwr
        dbr_ref[...] += dbr
        dwi_ref[...] += dwi
        dbi_ref[...] += dbi
        dlam_ref[...] += dlam
        dh[...] = jnp.broadcast_to(dhp, dh.shape)

    vec = pl.BlockSpec((1, LANE), lambda g, s: (0, g))
    mat = pl.BlockSpec((None, LANE, LANE), lambda g, s: (g, 0, 0))
    cwsp = pl.BlockSpec((4, LANE), lambda g, s: (0, g))
    in_specs = [pl.BlockSpec((tb, LANE), lambda g, s: (nb - 1 - s, LRU_X_OFF + g)),
                pl.BlockSpec((SUB, LANE), _tail_map_rev(tb, nb, lambda g: LRU_X_OFF + g)),
                pl.BlockSpec((tb, LANE), lambda g, s: (nb - 1 - s, LRU_G_OFF + g)),
                cwsp, vec, mat, vec, mat, vec, vec,
                pl.BlockSpec((None, None, SUB, LANE), lambda g, s: (g, nb - 1 - s, 0, 0)),
                pl.BlockSpec((tb, LANE), lambda g, s: (nb - 1 - s, LRU_BLOCKS + g))]
    oblk = pl.BlockSpec((tb, LANE), lambda g, s: (nb - 1 - s, g))
    out_specs = (oblk, oblk, cwsp, vec, mat, vec, mat, vec, vec)
    act = jax.ShapeDtypeStruct((t, LRU_WIDTH), BF16)
    vsh = jax.ShapeDtypeStruct((1, LRU_WIDTH), F32)
    msh = jax.ShapeDtypeStruct((g8, LANE, LANE), F32)
    out_shape = (act, act, jax.ShapeDtypeStruct((4, LRU_WIDTH), F32), vsh, msh, vsh, msh, vsh, vsh)
    return pl.pallas_call(
        body, name="lru_bwd", grid=(g8, nb), in_specs=in_specs, out_specs=out_specs, out_shape=out_shape,
        scratch_shapes=[pltpu.VMEM((SUB, LANE), F32), pltpu.VMEM((SUB, LANE), F32)], compiler_params=_params(2))(
            p, p, p, cw, cb, wr, br, wi, bi, lam, hs, dmix)


SWA_GROUPS = 4
SWA_QW = SWA_Q // SWA_GROUPS
SWA_K_OFF = SWA_Q // LANE
SWA_V_OFF = (SWA_Q + SWA_KV) // LANE


def rope_tables(pos):
    t = pos.shape[0]
    tb = _tile(t, (512, 256, 128))
    inv = 1.0 / (ROPE_THETA ** (jnp.arange(0, SWA_HD, 2, dtype=F32) / SWA_HD))
    inv = jnp.concatenate([inv, inv]).reshape(1, SWA_HD)

    def body(p_ref, i_ref, c_ref, s_ref):
        ang = p_ref[...] * i_ref[...]
        c_ref[...] = jnp.cos(ang)
        s_ref[...] = jnp.sin(ang)

    blk = pl.BlockSpec((tb, SWA_HD), lambda i: (i, 0))
    sh = jax.ShapeDtypeStruct((t, SWA_HD), F32)
    return pl.pallas_call(
        body, name="rope_tables", grid=(t // tb,),
        in_specs=[pl.BlockSpec((tb, 1), lambda i: (i, 0)), pl.BlockSpec((1, SWA_HD), lambda i: (0, 0))],
        out_specs=(blk, blk), out_shape=(sh, sh), compiler_params=_params(1))(pos, inv)


def _rot_matrix():
    r, c = _rows((SWA_HD, SWA_HD)), _lanes((SWA_HD, SWA_HD))
    half = SWA_HD // 2
    return jnp.where(r == c + half, -1.0, 0.0) + jnp.where(r + half == c, 1.0, 0.0)


def _swa_specs(tb, cur, prev):
    def q_sp():
        return pl.BlockSpec((tb, SWA_QW), lambda g, s: (cur(s), g))

    def kv(off, which):
        return pl.BlockSpec((tb, LANE), lambda g, s: (which(s), off + g))

    def tab(which):
        return pl.BlockSpec((tb, SWA_HD), lambda g, s: (which(s), 0))

    nrm = pl.BlockSpec((1, SWA_HD), lambda g, s: (0, 0))
    snk = pl.BlockSpec((1, LANE), lambda g, s: (0, 0))
    return [q_sp(), kv(SWA_K_OFF, cur), kv(SWA_K_OFF, prev), kv(SWA_V_OFF, cur), kv(SWA_V_OFF, prev),
            tab(cur), tab(cur), tab(prev), tab(prev), nrm, nrm, snk]


def swa_fwd(p, cos, sin, qn, kn, sinks):
    t = p.shape[0]
    tb = SWA_BLOCK
    nb = t // tb
    per = SWA_HEADS // SWA_KV_HEADS

    def body(q_ref, kc_ref, kp_ref, vc_ref, vp_ref, cc_ref, sc_ref, cp_ref, sp_ref, qn_ref, kn_ref, sk_ref, o_ref):
        g, ti = pl.program_id(0), pl.program_id(1)
        rot = _rot_matrix()
        for j in range(2):
            ksl = slice(j * SWA_HD, (j + 1) * SWA_HD)
            qs = [q_ref[:, (j * per + i) * SWA_HD:(j * per + i + 1) * SWA_HD] for i in range(per)]
            outs = _swa_kvhead(qs, kc_ref[:, ksl], kp_ref[:, ksl], vc_ref[:, ksl], vp_ref[:, ksl], cc_ref[...], sc_ref[...],
                               cp_ref[...], sp_ref[...], qn_ref[...], kn_ref[...], sk_ref[...],
                               g * (2 * per) + j * per, ti > 0, rot)
            for i in range(per):
                o_ref[:, (j * per + i) * SWA_HD:(j * per + i + 1) * SWA_HD] = outs[i].astype(o_ref.dtype)

    in_specs = _swa_specs(tb, lambda s: s, lambda s: jnp.maximum(s - 1, 0))
    return pl.pallas_call(
        body, name="swa_fwd", grid=(SWA_GROUPS, nb), in_specs=in_specs,
        out_specs=pl.BlockSpec((tb, SWA_QW), lambda g, s: (s, g)), out_shape=jax.ShapeDtypeStruct((t, SWA_Q), BF16),
        compiler_params=_params(2))(p, p, p, p, p, cos, sin, cos, sin, qn, kn, sinks)


def swa_bwd(p, cos, sin, qn, kn, sinks, do):
    t = p.shape[0]
    tb = SWA_BLOCK
    nb = t // tb
    per = SWA_HEADS // SWA_KV_HEADS

    def body(q_ref, kc_ref, kp_ref, vc_ref, vp_ref, cc_ref, sc_ref, cp_ref, sp_ref, qn_ref, kn_ref, sk_ref, do_ref,
             dq_ref, dk_ref, dv_ref, dqn_ref, dkn_ref, dsk_ref, ck, cv):
        g, step = pl.program_id(0), pl.program_id(1)
        ti = nb - 1 - step
        rot = _rot_matrix()

        @pl.when(step == 0)
        def _():
            ck[...] = jnp.zeros_like(ck)
            cv[...] = jnp.zeros_like(cv)

        @pl.when((step == 0) & (g == 0))
        def _():
            dqn_ref[...] = jnp.zeros_like(dqn_ref)
            dkn_ref[...] = jnp.zeros_like(dkn_ref)
            dsk_ref[...] = jnp.zeros_like(dsk_ref)

        for j in range(2):
            ksl = slice(j * SWA_HD, (j + 1) * SWA_HD)
            qs = [q_ref[:, (j * per + i) * SWA_HD:(j * per + i + 1) * SWA_HD] for i in range(per)]
            dos = tuple(do_ref[:, (j * per + i) * SWA_HD:(j * per + i + 1) * SWA_HD].astype(F32) for i in range(per))
            cc, sc, cp, sp = cc_ref[...], sc_ref[...], cp_ref[...], sp_ref[...]
            head0 = g * (2 * per) + j * per

            def fn(qs_, kc, kp, vc, vp, qn_, kn_, sk):
                return _swa_kvhead(qs_, kc, kp, vc, vp, cc, sc, cp, sp, qn_, kn_, sk, head0, ti > 0, rot)

            _, vjp = jax.vjp(fn, qs, kc_ref[:, ksl], kp_ref[:, ksl], vc_ref[:, ksl], vp_ref[:, ksl], qn_ref[...],
                             kn_ref[...], sk_ref[...])
            dqs, dkc, dkp, dvc, dvp, dqn, dkn, dsk = vjp(dos)
            for i in range(per):
                dq_ref[:, (j * per + i) * SWA_HD:(j * per + i + 1) * SWA_HD] = dqs[i].astype(dq_ref.dtype)
            dk_ref[:, ksl] = (dkc + ck[:, ksl]).astype(dk_ref.dtype)
            dv_ref[:, ksl] = (dvc + cv[:, ksl]).astype(dv_ref.dtype)
            ck[:, ksl] = dkp
            cv[:, ksl] = dvp
            dqn_ref[...] += dqn
            dkn_ref[...] += dkn
            dsk_ref[...] += dsk

    in_specs = _swa_specs(tb, lambda s: nb - 1 - s, lambda s: jnp.maximum(nb - 2 - s, 0))
    in_specs.append(pl.BlockSpec((tb, SWA_QW), lambda g, s: (nb - 1 - s, g)))
    kvo = pl.BlockSpec((tb, LANE), lambda g, s: (nb - 1 - s, g))
    nrm = pl.BlockSpec((1, SWA_HD), lambda g, s: (0, 0))
    out_specs = (pl.BlockSpec((tb, SWA_QW), lambda g, s: (nb - 1 - s, g)), kvo, kvo, nrm, nrm,
                 pl.BlockSpec((1, LANE), lambda g, s: (0, 0)))
    out_shape = (jax.ShapeDtypeStruct((t, SWA_Q), BF16), jax.ShapeDtypeStruct((t, SWA_KV), BF16),
                 jax.ShapeDtypeStruct((t, SWA_KV), BF16), jax.ShapeDtypeStruct((1, SWA_HD), F32),
                 jax.ShapeDtypeStruct((1, SWA_HD), F32), jax.ShapeDtypeStruct((1, LANE), F32))
    return pl.pallas_call(
        body, name="swa_bwd", grid=(SWA_GROUPS, nb), in_specs=in_specs, out_specs=out_specs, out_shape=out_shape,
        scratch_shapes=[pltpu.VMEM((tb, LANE), F32), pltpu.VMEM((tb, LANE), F32)], compiler_params=_params(2))(
            p, p, p, p, p, cos, sin, cos, sin, qn, kn, sinks, do)


def xattn_fwd(q, kv, qn, kn):
    t = q.shape[0]
    mlen = kv.shape[0]
    tb = _tile(t, (512, 256, 128, 64))

    def body(q_ref, k_ref, v_ref, qn_ref, kn_ref, o_ref):
        o_ref[...] = _xattn_head(q_ref[...], k_ref[...], v_ref[...], qn_ref[...], kn_ref[...]).astype(o_ref.dtype)

    nrm = pl.BlockSpec((1, X_HD), lambda g, s: (0, 0))
    in_specs = [pl.BlockSpec((tb, X_HD), lambda g, s: (s, g)), pl.BlockSpec((mlen, X_HD), lambda g, s: (0, g)),
                pl.BlockSpec((mlen, X_HD), lambda g, s: (0, X_HEADS + g)), nrm, nrm]
    return pl.pallas_call(
        body, name="xattn_fwd", grid=(X_HEADS, t // tb), in_specs=in_specs,
        out_specs=pl.BlockSpec((tb, X_HD), lambda g, s: (s, g)), out_shape=jax.ShapeDtypeStruct((t, X_INNER), BF16),
        compiler_params=_params(2))(q, kv, kv, qn, kn)


def xattn_bwd(q, kv, qn, kn, do):
    t = q.shape[0]
    mlen = kv.shape[0]
    tb = _tile(t, (512, 256, 128, 64))

    def body(q_ref, k_ref, v_ref, qn_ref, kn_ref, do_ref, dq_ref, dk_ref, dv_ref, dqn_ref, dkn_ref):
        g, s = pl.program_id(0), pl.program_id(1)

        @pl.when(s == 0)
        def _():
            dk_ref[...] = jnp.zeros_like(dk_ref)
            dv_ref[...] = jnp.zeros_like(dv_ref)

        @pl.when((s == 0) & (g == 0))
        def _():
            dqn_ref[...] = jnp.zeros_like(dqn_ref)
            dkn_ref[...] = jnp.zeros_like(dkn_ref)

        _, vjp = jax.vjp(_xattn_head, q_ref[...], k_ref[...], v_ref[...], qn_ref[...], kn_ref[...])
        dq, dk, dv, dqn, dkn = vjp(do_ref[...].astype(F32))
        dq_ref[...] = dq.astype(dq_ref.dtype)
        dk_ref[...] += dk
        dv_ref[...] += dv
        dqn_ref[...] += dqn
        dkn_ref[...] += dkn

    nrm = pl.BlockSpec((1, X_HD), lambda g, s: (0, 0))
    qblk = pl.BlockSpec((tb, X_HD), lambda g, s: (s, g))
    kblk = pl.BlockSpec((mlen, X_HD), lambda g, s: (0, g))
    in_specs = [qblk, kblk, pl.BlockSpec((mlen, X_HD), lambda g, s: (0, X_HEADS + g)), nrm, nrm, qblk]
    out_specs = (qblk, kblk, kblk, nrm, nrm)
    msh = jax.ShapeDtypeStruct((mlen, X_INNER), F32)
    nsh = jax.ShapeDtypeStruct((1, X_HD), F32)
    return pl.pallas_call(
        body, name="xattn_bwd", grid=(X_HEADS, t // tb), in_specs=in_specs, out_specs=out_specs,
        out_shape=(jax.ShapeDtypeStruct((t, X_INNER), BF16), msh, msh, nsh, nsh), compiler_params=_params(2))(
            q, kv, kv, qn, kn, do)


FFN_CW = 512
FFN_TB = 512


def ffn_fwd(gu, cw, cb):
    t, f2 = gu.shape
    f = f2 // 2
    cwid = _tile(f, (FFN_CW, 256, 128))
    ng = f // cwid
    tb = min(FFN_TB, t)

    def body(g_ref, tg_ref, u_ref, cw_ref, cb_ref, o_ref):
        live = (pl.program_id(1) > 0).astype(F32)
        o_ref[...] = _ffn_block(g_ref[...], tg_ref[...] * live, u_ref[...], cw_ref[...], cb_ref[...]).astype(o_ref.dtype)

    in_specs = [pl.BlockSpec((tb, cwid), lambda g, s: (s, g)), pl.BlockSpec((SUB, cwid), _tail_map(tb, lambda g: g)),
                pl.BlockSpec((tb, cwid), lambda g, s: (s, ng + g)), pl.BlockSpec((3, cwid), lambda g, s: (0, g)),
                pl.BlockSpec((1, cwid), lambda g, s: (0, g))]
    return pl.pallas_call(
        body, name="ffn_fwd", grid=(ng, t // tb), in_specs=in_specs, out_specs=pl.BlockSpec((tb, cwid), lambda g, s: (s, g)),
        out_shape=jax.ShapeDtypeStruct((t, f), BF16), compiler_params=_params(2))(gu, gu, gu, cw, cb)


def ffn_bwd(gu, cw, cb, da):
    t, f2 = gu.shape
    f = f2 // 2
    cwid = _tile(f, (FFN_CW, 256, 128))
    ng = f // cwid
    tb = min(FFN_TB, t)
    nb = t // tb

    def body(g_ref, tg_ref, u_ref, cw_ref, cb_ref, da_ref, dg_ref, du_ref, dcw_ref, dcb_ref, cg):
        step = pl.program_id(1)
        ti = nb - 1 - step

        @pl.when(step == 0)
        def _():
            cg[...] = jnp.zeros_like(cg)
            dcw_ref[...] = jnp.zeros_like(dcw_ref)
            dcb_ref[...] = jnp.zeros_like(dcb_ref)

        live = (ti > 0).astype(F32)
        _, vjp = jax.vjp(_ffn_block, g_ref[...], tg_ref[...] * live, u_ref[...], cw_ref[...], cb_ref[...])
        dg, dtg, du, dcw, dcb = vjp(da_ref[...].astype(F32))
        pad = jnp.zeros((tb - SUB, cwid), F32)
        dg_ref[...] = (dg + jnp.concatenate([pad, cg[...]], axis=0)).astype(dg_ref.dtype)
        cg[...] = dtg * live
        du_ref[...] = du.astype(du_ref.dtype)
        dcw_ref[...] += dcw
        dcb_ref[...] += dcb

    blk = pl.BlockSpec((tb, cwid), lambda g, s: (nb - 1 - s, g))
    wsp = pl.BlockSpec((3, cwid), lambda g, s: (0, g))
    bsp = pl.BlockSpec((1, cwid), lambda g, s: (0, g))
    in_specs = [blk, pl.BlockSpec((SUB, cwid), _tail_map_rev(tb, nb, lambda g: g)),
                pl.BlockSpec((tb, cwid), lambda g, s: (nb - 1 - s, ng + g)), wsp, bsp, blk]
    act = jax.ShapeDtypeStruct((t, f), BF16)
    return pl.pallas_call(
        body, name="ffn_bwd", grid=(ng, nb), in_specs=in_specs, out_specs=(blk, blk, wsp, bsp),
        out_shape=(act, act, jax.ShapeDtypeStruct((3, f), F32), jax.ShapeDtypeStruct((1, f), F32)),
        scratch_shapes=[pltpu.VMEM((SUB, cwid), F32)], compiler_params=_params(2))(gu, gu, gu, cw, cb, da)


def adamw(parts, w, m, v, name="adamw"):
    _, r, c = parts.shape
    tr = _tile(r, (256, 128, 64, 32, 16, 8))

    def body(p_ref, w_ref, m_ref, v_ref, g_ref, d_ref, nm_ref, nv_ref):
        g = p_ref[0]
        for s in range(1, N_DEV):
            g = g + p_ref[s]
        m_new = ADAM_B1 * m_ref[...] + (1.0 - ADAM_B1) * g
        v_new = ADAM_B2 * v_ref[...] + (1.0 - ADAM_B2) * jnp.square(g)
        m_hat = m_new / (1.0 - ADAM_B1 ** ADAM_STEP)
        v_hat = v_new / (1.0 - ADAM_B2 ** ADAM_STEP)
        g_ref[...] = g
        d_ref[...] = -ADAM_LR * (m_hat / (jnp.sqrt(v_hat) + ADAM_EPS) + ADAM_WD * w_ref[...])
        nm_ref[...] = m_new
        nv_ref[...] = v_new

    blk = pl.BlockSpec((tr, c), lambda i: (i, 0))
    sh = jax.ShapeDtypeStruct((r, c), F32)
    return pl.pallas_call(
        body, name=name, grid=(r // tr,), in_specs=[pl.BlockSpec((N_DEV, tr, c), lambda i: (0, i, 0)), blk, blk, blk],
        out_specs=(blk, blk, blk, blk), out_shape=(sh, sh, sh, sh), compiler_params=_params(1))(parts, w, m, v)


def _me():
    return lax.axis_index("x"), lax.axis_index("y"), lax.axis_index("c")


def _flip(coords, k):
    x, y, c = coords
    return (1 - x if k & 4 else x, 1 - y if k & 2 else y, 1 - c if k & 1 else c)


def _slot(coords):
    x, y, c = coords
    return 4 * x + 2 * y + c


def all_gather(shard, name):
    r, c = shard.shape

    def body(x_ref, out_ref, send_sems, recv_sems, local_sem):
        me = _me()
        sibling = _flip(me, 1)
        chips = [2, 4, 6]

        def copy(k, block, to, src=None):
            dst = out_ref.at[_slot(block)]
            return pltpu.make_async_remote_copy(
                src_ref=dst if src is None else src, dst_ref=dst, send_sem=send_sems.at[k], recv_sem=recv_sems.at[k],
                device_id=to, device_id_type=pl.DeviceIdType.MESH)

        mine = pltpu.make_async_copy(x_ref, out_ref.at[_slot(me)], local_sem)
        mine.start()
        first = [copy(0, me, sibling, src=x_ref)]
        first += [copy(1 + j, me, _flip(me, k), src=x_ref) for j, k in enumerate(chips)]
        for cp in first:
            cp.start()
        passed = [copy(4 + j, _flip(me, k), sibling) for j, k in enumerate(chips)]
        for j, k in enumerate(chips):
            copy(1 + j, _flip(me, k), me).wait_recv()
            passed[j].start()
        copy(0, sibling, me).wait_recv()
        for j, k in enumerate(chips):
            copy(4 + j, _flip(sibling, k), me).wait_recv()
        for cp in first + passed:
            cp.wait_send()
        mine.wait()

    return pl.pallas_call(
        body, name=name, out_shape=jax.ShapeDtypeStruct((N_DEV, r, c), shard.dtype),
        in_specs=[pl.BlockSpec(memory_space=pl.ANY)], out_specs=pl.BlockSpec(memory_space=pl.ANY),
        scratch_shapes=[pltpu.SemaphoreType.DMA((7,)), pltpu.SemaphoreType.DMA((7,)), pltpu.SemaphoreType.DMA(())],
    )(shard)


def all_to_all(blocks, name):
    _, r, c = blocks.shape

    def body(x_ref, out_ref, send_sems, recv_sems, local_sem):
        me = _me()
        my_slot = _slot(me)
        mine = pltpu.make_async_copy(x_ref.at[my_slot], out_ref.at[my_slot], local_sem)
        mine.start()
        copies = []
        for k in range(1, N_DEV):
            peer = _flip(me, k)
            copies.append(pltpu.make_async_remote_copy(
                src_ref=x_ref.at[_slot(peer)], dst_ref=out_ref.at[my_slot], send_sem=send_sems.at[k - 1],
                recv_sem=recv_sems.at[k - 1], device_id=peer, device_id_type=pl.DeviceIdType.MESH))
        for cp in copies:
            cp.start()
        for k in range(1, N_DEV):
            peer = _flip(me, k)
            pltpu.make_async_remote_copy(
                src_ref=x_ref.at[my_slot], dst_ref=out_ref.at[_slot(peer)], send_sem=send_sems.at[k - 1],
                recv_sem=recv_sems.at[k - 1], device_id=peer, device_id_type=pl.DeviceIdType.MESH).wait_recv()
        for cp in copies:
            cp.wait_send()
        mine.wait()

    return pl.pallas_call(
        body, name=name, out_shape=jax.ShapeDtypeStruct(blocks.shape, blocks.dtype),
        in_specs=[pl.BlockSpec(memory_space=pl.ANY)], out_specs=pl.BlockSpec(memory_space=pl.ANY),
        scratch_shapes=[pltpu.SemaphoreType.DMA((7,)), pltpu.SemaphoreType.DMA((7,)), pltpu.SemaphoreType.DMA(())],
    )(blocks)


def _pack_rows(n):
    rows = -(-n // PACK_COLS)
    return -(-rows // 16) * 16


def _pack_flat(pieces, dtype):
    flat = jnp.concatenate([p.reshape(-1).astype(dtype) for p in pieces])
    rows = _pack_rows(flat.shape[0])
    return jnp.pad(flat, (0, rows * PACK_COLS - flat.shape[0])).reshape(rows, PACK_COLS)


def _unpack_flat(buf, shapes):
    lead = buf.shape[:-2]
    flat = buf.reshape(lead + (-1,))
    out, off = [], 0
    for sh in shapes:
        n = int(np.prod(sh))
        out.append(flat[..., off:off + n].reshape(lead + tuple(sh)))
        off += n
    return out


def _merge(gathered, axis):
    if axis == 0:
        return gathered.reshape(-1, gathered.shape[2])
    return jnp.transpose(gathered, (1, 0, 2)).reshape(gathered.shape[1], -1)


def _split(full, axis):
    a, b = full.shape
    if axis == 0:
        return full.reshape(N_DEV, a // N_DEV, b)
    return jnp.transpose(full.reshape(a, N_DEV, b // N_DEV), (1, 0, 2))


def _f32_as_bf16(w):
    return lax.bitcast_convert_type(w, BF16)


def _bf16_as_f32(w):
    return lax.bitcast_convert_type(w, F32)


SHARDED = {
    "xq_w": 0, "xkv_w": 0, "xo_w": 1, "ffn_in_w": 1, "ffn_conv_w": 1, "ffn_out_w": 0,
    "hyb_in_w": 1, "hyb_out_w": 0, "gdn_conv_w": 1, "lru_conv_w": 1, "swa_in_w": 1, "swa_out_w": 0,
}
F32_SHIPPED = ("ffn_conv_w", "gdn_conv_w", "lru_conv_w")
COMMON = ("xq_w", "xkv_w", "xo_w", "ffn_in_w", "ffn_conv_w", "ffn_out_w")
EVEN = ("hyb_in_w", "hyb_out_w", "gdn_conv_w", "lru_conv_w")
ODD = ("swa_in_w", "swa_out_w")
WEIGHTS = ["norm_mix", "norm_cross", "norm_mem", "norm_ffn", "xq_w", "xkv_w", "xo_w", "xq_norm", "xk_norm", "ffn_in_w",
           "ffn_conv_w", "ffn_conv_b", "ffn_out_w", "hyb_in_w", "hyb_out_w", "gdn_conv_w", "gdn_a_log", "gdn_dt_bias",
           "gdn_norm", "lru_conv_w", "lru_conv_b", "lru_wr", "lru_br", "lru_wi", "lru_bi", "lru_lambda", "swa_in_w",
           "swa_out_w", "swa_q_norm", "swa_k_norm", "swa_sinks"]
REPLICATED = [n for n in WEIGHTS if n not in SHARDED]


def _layer_names(l):
    return COMMON + (EVEN if l % 2 == 0 else ODD)


PER_LAYER = COMMON + ("norm_mix", "norm_cross", "norm_mem", "norm_ffn", "xq_norm", "xk_norm", "ffn_conv_b")


def _layer_index(name, l):
    return l if name in PER_LAYER else l // 2


def _gather_layer(w, l):
    names = _layer_names(l)
    pieces, shapes = [], []
    for n in names:
        sh = w[n][_layer_index(n, l)]
        if n in F32_SHIPPED:
            sh = _f32_as_bf16(sh)
        pieces.append(sh)
        shapes.append(sh.shape)
    gathered = all_gather(_pack_flat(pieces, BF16), name=f"gather_w{l % 2}")
    full = {}
    for n, g in zip(names, _unpack_flat(gathered, shapes)):
        if n in F32_SHIPPED:
            g = _bf16_as_f32(g)
        full[n] = _merge(g, SHARDED[n])
    return full


def _hyb_pad(w_full):
    d = w_full.shape[0]
    return jnp.concatenate([w_full[:, :4096], w_full[:, 4112:HYB_IN], w_full[:, 4096:4112],
                            jnp.zeros((d, HYB_PAD - HYB_IN), w_full.dtype)], axis=1)


def _hyb_unpad(dw):
    return jnp.concatenate([dw[:, :4096], dw[:, 6144:6160], dw[:, 4096:6144]], axis=1)


def _pad_lanes(v, n=LANE):
    return jnp.pad(v.reshape(1, -1), ((0, 0), (0, n - v.shape[-1])))


def _layer_fwd(l, x, mem, full, w, cos, sin):
    e = l // 2
    sv = {"x": x}
    g_mix = w["norm_mix"][l:l + 1]
    h1 = norm_fwd(x, g_mix)
    sv["h1"] = h1
    if l % 2 == 0:
        w_in = _hyb_pad(full["hyb_in_w"])
        p = matmul(h1, w_in, "nn", name="mm_hyb_in")
        sc = jnp.concatenate([_pad_lanes(w["gdn_a_log"][e]), _pad_lanes(w["gdn_dt_bias"][e]), jnp.zeros((6, LANE), F32)], 0)
        ng = w["gdn_norm"][e:e + 1]
        oa, states = gdn_fwd(p, full["gdn_conv_w"], sc, ng)
        lru_args = (full["lru_conv_w"], w["lru_conv_b"][e:e + 1], w["lru_wr"][e], w["lru_br"][e:e + 1], w["lru_wi"][e],
                    w["lru_bi"][e:e + 1], w["lru_lambda"][e:e + 1])
        ob, hs = lru_fwd(p, *lru_args)
        w_out = full["hyb_out_w"]
        x1 = matmul(oa, w_out[:GDN_V], "nn", res=x, name="mm_hyb_out_a")
        x1 = matmul(ob, w_out[GDN_V:], "nn", res=x1, name="mm_hyb_out_b")
        sv.update(w_in=w_in, p=p, sc=sc, ng=ng, states=states, lru_args=lru_args, hs=hs, oa=oa, ob=ob)
    else:
        p = matmul(h1, full["swa_in_w"], "nn", name="mm_swa_in")
        swa_args = (cos, sin, w["swa_q_norm"][e:e + 1], w["swa_k_norm"][e:e + 1], _pad_lanes(w["swa_sinks"][e]))
        o = swa_fwd(p, *swa_args)
        x1 = matmul(o, full["swa_out_w"], "nn", res=x, name="mm_swa_out")
        sv.update(p=p, swa_args=swa_args, o=o)
    hc = norm_fwd(x1, w["norm_cross"][l:l + 1])
    memn = norm_fwd(mem, w["norm_mem"][l:l + 1], name="norm_mem_fwd")
    q = matmul(hc, full["xq_w"], "nn", name="mm_xq")
    kv = matmul(memn, full["xkv_w"], "nn", name="mm_xkv")
    qn, kn = w["xq_norm"][l:l + 1], w["xk_norm"][l:l + 1]
    ox = xattn_fwd(q, kv, qn, kn)
    x2 = matmul(ox, full["xo_w"], "nn", res=x1, name="mm_xo")
    hf = norm_fwd(x2, w["norm_ffn"][l:l + 1])
    gu = matmul(hf, full["ffn_in_w"], "nn", name="mm_ffn_in")
    cb = w["ffn_conv_b"][l:l + 1]
    a = ffn_fwd(gu, full["ffn_conv_w"], cb)
    x3 = matmul(a, full["ffn_out_w"], "nn", res=x2, name="mm_ffn_out")
    sv.update(x1=x1, hc=hc, memn=memn, q=q, kv=kv, qn=qn, kn=kn, ox=ox, x2=x2, hf=hf, gu=gu, cb=cb, a=a)
    return x3, sv


def _layer_bwd(l, dx3, mem, full, w, sv):
    e = l // 2
    gs, gr = {}, {}
    f = full["ffn_out_w"].shape[0]
    da = matmul(dx3, full["ffn_out_w"], "nt", name="mm_ffn_out_da")
    gs["ffn_out_w"] = matmul(sv["a"], dx3, "tn", name="mm_ffn_out_dw")
    dgate, dup, dcw, dcb = ffn_bwd(sv["gu"], full["ffn_conv_w"], sv["cb"], da)
    gs["ffn_conv_w"], gr["ffn_conv_b"] = dcw, dcb
    w_in = full["ffn_in_w"]
    dhf = matmul(dgate, w_in[:, :f], "nt", name="mm_ffn_in_dh_g")
    dhf = matmul(dup, w_in[:, f:], "nt", res=dhf, name="mm_ffn_in_dh_u")
    gs["ffn_in_w"] = jnp.concatenate([matmul(sv["hf"], dgate, "tn", name="mm_ffn_in_dw_g"),
                                      matmul(sv["hf"], dup, "tn", name="mm_ffn_in_dw_u")], axis=1)
    dx2, gr["norm_ffn"] = norm_bwd(sv["x2"], w["norm_ffn"][l:l + 1], dhf, dx3)
    dox = matmul(dx2, full["xo_w"], "nt", name="mm_xo_do")
    gs["xo_w"] = matmul(sv["ox"], dx2, "tn", name="mm_xo_dw")
    dq, dk, dv, gr["xq_norm"], gr["xk_norm"] = xattn_bwd(sv["q"], sv["kv"], sv["qn"], sv["kn"], dox)
    dkv = jnp.concatenate([dk, dv], axis=1)
    dhc = matmul(dq, full["xq_w"], "nt", name="mm_xq_dh")
    gs["xq_w"] = matmul(sv["hc"], dq, "tn", name="mm_xq_dw")
    dmemn = matmul(dkv, full["xkv_w"], "nt", name="mm_xkv_dh")
    gs["xkv_w"] = matmul(sv["memn"], dkv, "tn", name="mm_xkv_dw")
    gr["norm_mem"] = norm_bwd(mem, w["norm_mem"][l:l + 1], dmemn, name="norm_mem_bwd")
    dx1, gr["norm_cross"] = norm_bwd(sv["x1"], w["norm_cross"][l:l + 1], dhc, dx2)
    if l % 2 == 0:
        w_out = full["hyb_out_w"]
        dmix = matmul(dx1, w_out, "nt", name="mm_hyb_out_dm")
        gs["hyb_out_w"] = jnp.concatenate([matmul(sv["oa"], dx1, "tn", name="mm_hyb_out_dw_a"),
                                           matmul(sv["ob"], dx1, "tn", name="mm_hyb_out_dw_b")], axis=0)
        dq_, dk_, dv_, dz, dba, dwq, dwk, dwv, dsc, dng = gdn_bwd(
            sv["p"], full["gdn_conv_w"], sv["sc"], sv["ng"], sv["states"], dmix)
        dlx, dlg, dcw, dcb, dwr, dbr, dwi, dbi, dlam = lru_bwd(sv["p"], *sv["lru_args"], sv["hs"], dmix)
        dba = jnp.sum(dba, axis=0).astype(BF16)
        dp = jnp.concatenate([dq_, dk_, dv_, dz, dlx, dlg, dba], axis=1)
        dh1 = matmul(dp, sv["w_in"], "nt", name="mm_hyb_in_dh")
        gs["hyb_in_w"] = _hyb_unpad(matmul(sv["h1"], dp, "tn", name="mm_hyb_in_dw"))
        gs["gdn_conv_w"] = jnp.concatenate([dwq, dwk, dwv], axis=1)
        gs["lru_conv_w"] = dcw
        gr.update(gdn_a_log=dsc[0, :GDN_HEADS], gdn_dt_bias=dsc[1, :GDN_HEADS], gdn_norm=dng[0], lru_conv_b=dcb[0],
                  lru_wr=dwr, lru_br=dbr[0], lru_wi=dwi, lru_bi=dbi[0], lru_lambda=dlam[0])
    else:
        do = matmul(dx1, full["swa_out_w"], "nt", name="mm_swa_out_do")
        gs["swa_out_w"] = matmul(sv["o"], dx1, "tn", name="mm_swa_out_dw")
        dq_, dk_, dv_, dqn, dkn, dsk = swa_bwd(sv["p"], *sv["swa_args"], do)
        dp = jnp.concatenate([dq_, dk_, dv_], axis=1)
        dh1 = matmul(dp, full["swa_in_w"], "nt", name="mm_swa_in_dh")
        gs["swa_in_w"] = matmul(sv["h1"], dp, "tn", name="mm_swa_in_dw")
        gr.update(swa_q_norm=dqn[0], swa_k_norm=dkn[0], swa_sinks=dsk[0, :SWA_HEADS])
    dx, gr["norm_mix"] = norm_bwd(sv["x"], w["norm_mix"][l:l + 1], dh1, dx1)
    for n in ("norm_ffn", "norm_mem", "norm_cross", "norm_mix", "xq_norm", "xk_norm", "ffn_conv_b"):
        gr[n] = gr[n][0]
    return dx, gs, gr


def _update_layer(l, gs, w, m, v):
    names = _layer_names(l)
    blocks = [_split(gs[n], SHARDED[n]) for n in names]
    shapes = [b.shape[1:] for b in blocks]
    send = jnp.concatenate([b.reshape(N_DEV, -1) for b in blocks], axis=1)
    rows = _pack_rows(send.shape[1])
    send = jnp.pad(send, ((0, 0), (0, rows * PACK_COLS - send.shape[1]))).reshape(N_DEV, rows, PACK_COLS)
    parts = all_to_all(send, name=f"exchange_g{l % 2}")
    packed = [_pack_flat([t[n][_layer_index(n, l)] for n in names], F32) for t in (w, m, v)]
    outs = adamw(parts, *packed, name=f"adamw_l{l % 2}")
    per = [_unpack_flat(o, shapes) for o in outs]
    return {n: tuple(per[k][i] for k in range(4)) for i, n in enumerate(names)}


def kernel(x, mem, positions, norm_mix, norm_cross, norm_mem, norm_ffn, xq_w, xkv_w, xo_w, xq_norm, xk_norm, ffn_in_w, ffn_conv_w, ffn_conv_b, ffn_out_w, hyb_in_w, hyb_out_w, gdn_conv_w, gdn_a_log, gdn_dt_bias, gdn_norm, lru_conv_w, lru_conv_b, lru_wr, lru_br, lru_wi, lru_bi, lru_lambda, swa_in_w, swa_out_w, swa_q_norm, swa_k_norm, swa_sinks, loss_target, m_norm_mix, m_norm_cross, m_norm_mem, m_norm_ffn, m_xq_w, m_xkv_w, m_xo_w, m_xq_norm, m_xk_norm, m_ffn_in_w, m_ffn_conv_w, m_ffn_conv_b, m_ffn_out_w, m_hyb_in_w, m_hyb_out_w, m_gdn_conv_w, m_gdn_a_log, m_gdn_dt_bias, m_gdn_norm, m_lru_conv_w, m_lru_conv_b, m_lru_wr, m_lru_br, m_lru_wi, m_lru_bi, m_lru_lambda, m_swa_in_w, m_swa_out_w, m_swa_q_norm, m_swa_k_norm, m_swa_sinks, v_norm_mix, v_norm_cross, v_norm_mem, v_norm_ffn, v_xq_w, v_xkv_w, v_xo_w, v_xq_norm, v_xk_norm, v_ffn_in_w, v_ffn_conv_w, v_ffn_conv_b, v_ffn_out_w, v_hyb_in_w, v_hyb_out_w, v_gdn_conv_w, v_gdn_a_log, v_gdn_dt_bias, v_gdn_norm, v_lru_conv_w, v_lru_conv_b, v_lru_wr, v_lru_br, v_lru_wi, v_lru_bi, v_lru_lambda, v_swa_in_w, v_swa_out_w, v_swa_q_norm, v_swa_k_norm, v_swa_sinks):
    args = locals()
    w = {n: args[n] for n in WEIGHTS}
    m = {n: args["m_" + n] for n in WEIGHTS}
    v = {n: args["v_" + n] for n in WEIGHTS}
    depth = norm_mix.shape[0]
    xs = x[0]
    mems = mem[0]
    cos, sin = rope_tables(positions[0].astype(F32).reshape(-1, 1))

    saved, fulls = [], []
    h = xs
    for l in range(depth):
        full = _gather_layer(w, l)
        h, sv = _layer_fwd(l, h, mems, full, w, cos, sin)
        saved.append(sv)
        fulls.append(full)
    dy, lpart = loss_grad(h, loss_target[0])
    loss = lax.psum(lpart[0, 0], ("x", "y", "c"))

    res = {}
    rep = {n: [None] * w[n].shape[0] for n in REPLICATED}
    dh = dy
    for l in reversed(range(depth)):
        dh, gs, gr = _layer_bwd(l, dh, mems, fulls[l], w, saved[l])
        for n, g in gr.items():
            rep[n][_layer_index(n, l)] = g
        for n, quad in _update_layer(l, gs, w, m, v).items():
            res.setdefault(n, [None] * w[n].shape[0])[_layer_index(n, l)] = quad
    grad_x = dh[None]

    rep_shapes = [w[n].shape for n in REPLICATED]
    local = _pack_flat([jnp.stack(rep[n]).reshape(w[n].shape) for n in REPLICATED], F32)
    parts = all_gather(local, name="gather_rep")
    outs = adamw(parts, *[_pack_flat([t[n] for n in REPLICATED], F32) for t in (w, m, v)], name="adamw_rep")
    per = [_unpack_flat(o, rep_shapes) for o in outs]
    quads = {n: tuple(per[k][i] for k in range(4)) for i, n in enumerate(REPLICATED)}
    for n in SHARDED:
        quads[n] = tuple(jnp.stack([res[n][i][k] for i in range(w[n].shape[0])]) for k in range(4))
    return (loss, grad_x, *[quads[n][0] for n in WEIGHTS], *[quads[n][1] for n in WEIGHTS],
            *[quads[n][2] for n in WEIGHTS], *[quads[n][3] for n in WEIGHTS])
```

```python
import functools
import math

import jax
import jax.numpy as jnp
import numpy as np
from jax import lax
from jax.experimental import pallas as pl
from jax.experimental.pallas import tpu as pltpu

F32 = jnp.float32
BF16 = jnp.bfloat16
HIGHEST = lax.Precision.HIGHEST

EPS = 1e-6
N_DEV = 8
GDN_HEADS = 8
GDN_DK = 128
GDN_CHUNK = 64
GDN_QKV = 3072
GDN_V = 1024
LRU_WIDTH = 1024
LRU_BLOCKS = 8
LRU_C = 8.0
SWA_HEADS = 32
SWA_KV_HEADS = 8
SWA_HD = 64
SWA_BLOCK = 128
SWA_Q = SWA_HEADS * SWA_HD
SWA_KV = SWA_KV_HEADS * SWA_HD
ROPE_THETA = 10000.0
X_HEADS = 4
X_HD = 128
X_INNER = X_HEADS * X_HD
HYB_IN = 6160
HYB_PAD = 6272
LANE = 128
SUB = 8
ADAM_LR = 0.001
ADAM_B1 = 0.9
ADAM_B2 = 0.999
ADAM_EPS = 1e-08
ADAM_WD = 0.01
ADAM_STEP = 10
PACK_COLS = 1024
VMEM_LIMIT = 56 * 1024 * 1024


def _params(n_grid):
    return pltpu.CompilerParams(dimension_semantics=("arbitrary",) * n_grid, vmem_limit_bytes=VMEM_LIMIT)


def _dg(a, b, ca, cb, hi):
    dims = (((ca,), (cb,)), ((), ()))
    if hi:
        return lax.dot_general(a, b, dims, precision=HIGHEST, preferred_element_type=F32)
    return lax.dot_general(a.astype(BF16), b.astype(BF16), dims, preferred_element_type=F32)


@functools.partial(jax.custom_vjp, nondiff_argnums=(2, 3, 4))
def mm(a, b, ca, cb, hi):
    return _dg(a, b, ca, cb, hi)


def _mm_fwd(a, b, ca, cb, hi):
    return _dg(a, b, ca, cb, hi), (a, b)


def _mm_bwd(ca, cb, hi, res, g):
    a, b = res
    if ca == 1:
        da = _dg(g, b, 1, 1 - cb, hi)
    else:
        da = _dg(b, g, 1 - cb, 1, hi)
    if cb == 0:
        db = _dg(a, g, 1 - ca, 0, hi)
    else:
        db = _dg(g, a, 0, 1 - ca, hi)
    return da, db


mm.defvjp(_mm_fwd, _mm_bwd)


def mm_nn(a, b, hi=False):
    return mm(a, b, 1, 0, hi)


def mm_nt(a, b, hi=False):
    return mm(a, b, 1, 1, hi)


def mm_tn(a, b, hi=False):
    return mm(a, b, 0, 0, hi)


def _rows(shape):
    return lax.broadcasted_iota(jnp.int32, shape, 0)


def _lanes(shape):
    return lax.broadcasted_iota(jnp.int32, shape, 1)


@functools.partial(jax.custom_vjp, nondiff_argnums=(2,))
def shift_rows(x, tail, k):
    rx = pltpu.roll(x, k, 0)
    rt = pltpu.roll(tail, k, 0)
    first = jnp.where(_rows(tail.shape) < k, rt, rx[:SUB])
    return jnp.concatenate([first, rx[SUB:]], axis=0)


def _shift_rows_fwd(x, tail, k):
    return shift_rows(x, tail, k), None


def _shift_rows_bwd(k, _, g):
    n = g.shape[0]
    dx = jnp.where(_rows(g.shape) < n - k, pltpu.roll(g, n - k, 0), 0.0)
    g8 = g[:SUB]
    dtail = jnp.where(_rows(g8.shape) >= SUB - k, pltpu.roll(g8, SUB - k, 0), 0.0)
    return dx, dtail


shift_rows.defvjp(_shift_rows_fwd, _shift_rows_bwd)


@functools.partial(jax.custom_vjp, nondiff_argnums=(1, 2))
def shift_fill(x, k, fill):
    return jnp.where(_rows(x.shape) >= k, pltpu.roll(x, k, 0), fill)


def _shift_fill_fwd(x, k, fill):
    return shift_fill(x, k, fill), None


def _shift_fill_bwd(k, fill, _, g):
    n = g.shape[0]
    return (jnp.where(_rows(g.shape) < n - k, pltpu.roll(g, n - k, 0), 0.0),)


shift_fill.defvjp(_shift_fill_fwd, _shift_fill_bwd)


@jax.custom_vjp
def inv_unit_lower(l):
    n = l.shape[0]
    eye = (_rows(l.shape) == _lanes(l.shape)).astype(F32)
    t = eye - l
    p = l
    k = 2
    while k < n:
        p = _dg(p, p, 1, 0, True)
        t = _dg(t, eye + p, 1, 0, True)
        k *= 2
    return t


def _inv_fwd(l):
    t = inv_unit_lower(l)
    return t, t


def _inv_bwd(t, g):
    x = _dg(g, t, 1, 1, True)
    return (-_dg(t, x, 0, 0, True),)


inv_unit_lower.defvjp(_inv_fwd, _inv_bwd)


def _row_of(w, j):
    return jnp.sum(jnp.where(_rows(w.shape) == j, w, 0.0), axis=0, keepdims=True)


def _lane_col(x, j):
    return jnp.sum(jnp.where(_lanes(x.shape) == j, x, 0.0), axis=1, keepdims=True)


def _scalar_at(w, r, j):
    m = (_rows(w.shape) == r) & (_lanes(w.shape) == j)
    s = jnp.sum(jnp.where(m, w, 0.0), axis=1, keepdims=True)
    return jnp.sum(s, axis=0, keepdims=True)


def _col2row(col, n):
    eye = _rows((n, n)) == _lanes((n, n))
    return jnp.sum(jnp.where(eye, jnp.broadcast_to(col, (n, n)), 0.0), axis=0, keepdims=True)


def _rms(x, g):
    return x * lax.rsqrt(jnp.mean(x * x, axis=-1, keepdims=True) + EPS) * g


def _conv_core(x, tail, w, width):
    y = x * _row_of(w, width - 1)
    for k in range(1, width):
        y = y + shift_rows(x, tail, k) * _row_of(w, width - 1 - k)
    return y


def _bf16_round(a):
    return a.astype(BF16).astype(F32)


@functools.partial(jax.custom_vjp, nondiff_argnums=(3,))
def _conv(x, tail, w, width):
    return _conv_core(_bf16_round(x), _bf16_round(tail), _bf16_round(w), width)


def _conv_fwd(x, tail, w, width):
    xb, tb, wb = _bf16_round(x), _bf16_round(tail), _bf16_round(w)
    return _conv_core(xb, tb, wb, width), (xb, tb, wb)


def _conv_bwd(width, res, g):
    _, vjp = jax.vjp(functools.partial(_conv_core, width=width), *res)
    return vjp(_bf16_round(g))


_conv.defvjp(_conv_fwd, _conv_bwd)


def _neg_expm1(x):
    t = jnp.tanh(0.5 * x)
    return -2.0 * t / (1.0 - t)


def _gdn_chunk(qr, kr, vr, tq, tk, tv, z, ba, wq, wk, wv, sc, ng, state, head):
    c = GDN_CHUNK
    qc = jax.nn.silu(_conv(qr, tq, wq, 4))
    kc = jax.nn.silu(_conv(kr, tk, wk, 4))
    v = jax.nn.silu(_conv(vr, tv, wv, 4))
    q = qc * lax.rsqrt(jnp.sum(qc * qc, axis=-1, keepdims=True) + EPS) * (GDN_DK ** -0.5)
    k = kc * lax.rsqrt(jnp.sum(kc * kc, axis=-1, keepdims=True) + EPS)
    beta = jax.nn.sigmoid(_lane_col(ba, head))
    a_raw = _lane_col(ba, head + GDN_HEADS)
    a_log = _scalar_at(sc, 0, head)
    dt_bias = _scalar_at(sc, 1, head)
    gl = -jnp.exp(a_log) * jax.nn.softplus(a_raw + dt_bias)
    causal = _rows((c, c)) >= _lanes((c, c))
    strict = _rows((c, c)) > _lanes((c, c))
    gl_row = _col2row(gl, c)
    gc = jnp.sum(jnp.where(causal, gl_row, 0.0), axis=1, keepdims=True)
    gc_row = _col2row(gc, c)
    decay = jnp.exp(jnp.where(causal, gc - gc_row, -jnp.inf))
    kb = k * beta
    l_mat = jnp.where(strict, mm_nt(kb, k) * decay, 0.0)
    tmat = inv_unit_lower(l_mat)
    eg = jnp.exp(gc)
    u = mm_nn(tmat, v * beta)
    w = mm_nn(tmat, kb * eg)
    attn = jnp.where(causal, mm_nt(q, k) * decay, 0.0)
    q_dec = q * eg
    g_last = jnp.sum(gl, axis=0, keepdims=True)
    k_dec = k * jnp.exp(g_last - gc)
    v_new = u - mm_nn(w, state)
    o = mm_nn(q_dec, state) + mm_nn(attn, v_new)
    new_state = state * jnp.exp(g_last) + mm_tn(k_dec, v_new)
    return _rms(o, ng) * jax.nn.silu(z), new_state


def _lru_block(lx, tail, lg, cw, cb, wr, br, wi, bi, lam, hprev):
    n = lx.shape[0]
    xc = _conv(lx, tail, cw, 4) + cb
    r = jax.nn.sigmoid(mm_nn(xc, wr) + br)
    i = jax.nn.sigmoid(mm_nn(xc, wi) + bi)
    log_a = -LRU_C * r * jax.nn.softplus(-lam)
    a = jnp.exp(log_a)
    b = jnp.sqrt(_neg_expm1(2.0 * log_a)) * (i * xc)
    k = 1
    while k < n:
        b = a * shift_fill(b, k, 0.0) + b
        a = a * shift_fill(a, k, 1.0)
        k *= 2
    h = b + a * hprev
    hlast = jnp.sum(jnp.where(_rows(h.shape) == n - 1, h, 0.0), axis=0, keepdims=True)
    return h * jax.nn.gelu(lg), hlast


def _rope(x, cos, sin, rot):
    return x * cos + mm_nn(x, rot, True) * sin


def _swa_kvhead(qs, kc, kp, vc, vp, cosc, sinc, cosp, sinp, qn, kn, sinks, head0, not_first, rot):
    n = kc.shape[0]
    first_off = jnp.where(not_first, 0, n)
    scale = SWA_HD ** -0.5
    kc_r = _rope(_rms(kc, kn), cosc, sinc, rot)
    kp_r = _rope(_rms(kp, kn), cosp, sinp, rot)
    qi = _rows((n, n))
    kj = _lanes((n, n))
    outs = []
    for i, qh in enumerate(qs):
        q_r = _rope(_rms(qh, qn), cosc, sinc, rot)
        s_c = jnp.where(kj <= qi, mm_nt(q_r, kc_r) * scale, -jnp.inf)
        s_p = jnp.where(kj > qi + first_off, mm_nt(q_r, kp_r) * scale, -jnp.inf)
        sink = _scalar_at(sinks, 0, head0 + i)
        m = jnp.maximum(jnp.maximum(jnp.max(s_c, axis=1, keepdims=True), jnp.max(s_p, axis=1, keepdims=True)), sink)
        m = lax.stop_gradient(m)
        pc = jnp.exp(s_c - m)
        pp = jnp.exp(s_p - m)
        denom = jnp.sum(pc, axis=1, keepdims=True) + jnp.sum(pp, axis=1, keepdims=True) + jnp.exp(sink - m)
        outs.append(mm_nn(pc / denom, vc) + mm_nn(pp / denom, vp))
    return tuple(outs)


def _xattn_head(q, k, v, qn, kn):
    qh = _rms(q, qn)
    kh = _rms(k, kn)
    s = mm_nt(qh, kh) * (X_HD ** -0.5)
    m = lax.stop_gradient(jnp.max(s, axis=1, keepdims=True))
    p = jnp.exp(s - m)
    p = p / jnp.sum(p, axis=1, keepdims=True)
    return mm_nn(p, v)


def _ffn_block(gate, tail, up, cw, cb):
    gt = _conv(gate, tail, cw, 3) + cb
    return jax.nn.silu(gt) * up


def _tile(n, cands):
    for c in cands:
        if n % c == 0:
            return c
    return n


def matmul(a, b, mode, res=None, out_dtype=F32, name="mm"):
    if mode == "nn":
        (m, k), (_, n) = a.shape, b.shape
    elif mode == "nt":
        (m, k), (n, _) = a.shape, b.shape
    else:
        (k, m), (_, n) = a.shape, b.shape
    tm = _tile(m, (1024, 1408, 896, 512, 256, 128))
    tn = _tile(n, (1408, 1024, 896, 768, 640, 512, 384, 256, 128))
    tk = _tile(k, (512, 896, 256, 128))
    nk = k // tk
    ca, cb = {"nn": (1, 0), "nt": (1, 1), "tn": (0, 0)}[mode]
    a_spec = pl.BlockSpec((tk, tm), lambda i, j, kk: (kk, i)) if mode == "tn" else pl.BlockSpec((tm, tk), lambda i, j, kk: (i, kk))
    b_spec = pl.BlockSpec((tn, tk), lambda i, j, kk: (j, kk)) if mode == "nt" else pl.BlockSpec((tk, tn), lambda i, j, kk: (kk, j))
    o_spec = pl.BlockSpec((tm, tn), lambda i, j, kk: (i, j))
    has_res = res is not None

    def body(*refs):
        if has_res:
            a_ref, b_ref, r_ref, o_ref, acc = refs
        else:
            a_ref, b_ref, o_ref, acc = refs
        kk = pl.program_id(2)

        @pl.when(kk == 0)
        def _():
            acc[...] = jnp.zeros_like(acc)

        acc[...] += _dg(a_ref[...], b_ref[...], ca, cb, False)

        @pl.when(kk == nk - 1)
        def _():
            r = acc[...]
            if has_res:
                r = r + r_ref[...].astype(F32)
            o_ref[...] = r.astype(o_ref.dtype)

    ins = [a, b] + ([res] if has_res else [])
    specs = [a_spec, b_spec] + ([o_spec] if has_res else [])
    return pl.pallas_call(
        body, name=name, grid=(m // tm, n // tn, nk), in_specs=specs, out_specs=o_spec,
        out_shape=jax.ShapeDtypeStruct((m, n), out_dtype), scratch_shapes=[pltpu.VMEM((tm, tn), F32)],
        compiler_params=_params(3))(*ins)


def norm_fwd(x, g, name="norm_fwd"):
    t, d = x.shape
    tb = _tile(t, (512, 256, 128, 64))

    def body(x_ref, g_ref, o_ref):
        o_ref[...] = _rms(x_ref[...], g_ref[...]).astype(o_ref.dtype)

    return pl.pallas_call(
        body, name=name, grid=(t // tb,),
        in_specs=[pl.BlockSpec((tb, d), lambda i: (i, 0)), pl.BlockSpec((1, d), lambda i: (0, 0))],
        out_specs=pl.BlockSpec((tb, d), lambda i: (i, 0)), out_shape=jax.ShapeDtypeStruct((t, d), BF16),
        compiler_params=_params(1))(x, g)


def norm_bwd(x, g, dh, dres=None, name="norm_bwd"):
    t, d = x.shape
    tb = _tile(t, (256, 128, 64))
    with_dx = dres is not None

    def body(*refs):
        if with_dx:
            x_ref, g_ref, dh_ref, dr_ref, dx_ref, dg_ref = refs
        else:
            x_ref, g_ref, dh_ref, dg_ref = refs
        _, vjp = jax.vjp(_rms, x_ref[...], g_ref[...])
        dx, dg = vjp(dh_ref[...].astype(F32))

        @pl.when(pl.program_id(0) == 0)
        def _():
            dg_ref[...] = jnp.zeros_like(dg_ref)

        dg_ref[...] += dg
        if with_dx:
            dx_ref[...] = dx + dr_ref[...]

    blk = pl.BlockSpec((tb, d), lambda i: (i, 0))
    gsp = pl.BlockSpec((1, d), lambda i: (0, 0))
    if with_dx:
        return pl.pallas_call(
            body, name=name, grid=(t // tb,), in_specs=[blk, gsp, blk, blk], out_specs=(blk, gsp),
            out_shape=(jax.ShapeDtypeStruct((t, d), F32), jax.ShapeDtypeStruct((1, d), F32)),
            compiler_params=_params(1))(x, g, dh, dres)
    return pl.pallas_call(
        body, name=name, grid=(t // tb,), in_specs=[blk, gsp, blk], out_specs=gsp,
        out_shape=jax.ShapeDtypeStruct((1, d), F32), compiler_params=_params(1))(x, g, dh)


def loss_grad(y, target):
    t, d = y.shape
    tb = _tile(t, (512, 256, 128, 64))

    def body(y_ref, t_ref, dy_ref, l_ref):
        err = y_ref[...] - t_ref[...]
        dy_ref[...] = err / d

        @pl.when(pl.program_id(0) == 0)
        def _():
            l_ref[...] = jnp.zeros_like(l_ref)

        part = jnp.sum(jnp.mean(err * err, axis=-1, keepdims=True), axis=0, keepdims=True)
        l_ref[...] += 0.5 * part

    blk = pl.BlockSpec((tb, d), lambda i: (i, 0))
    lsp = pl.BlockSpec((SUB, LANE), lambda i: (0, 0))
    return pl.pallas_call(
        body, name="loss_grad", grid=(t // tb,), in_specs=[blk, blk], out_specs=(blk, lsp),
        out_shape=(jax.ShapeDtypeStruct((t, d), F32), jax.ShapeDtypeStruct((SUB, LANE), F32)),
        compiler_params=_params(1))(y, target)


def _tail_map(tb, col):
    return lambda g, t: (jnp.maximum(t * (tb // SUB) - 1, 0), col(g))


def _tail_map_rev(tb, nb, col):
    return lambda g, t: (jnp.maximum((nb - 1 - t) * (tb // SUB) - 1, 0), col(g))


GDN_HPB = 4

def gdn_fwd(p, conv_w, sc, ng):
    t = p.shape[0]
    tb = GDN_CHUNK
    nb = t // tb
    h = GDN_HEADS
    hpb = GDN_HPB
    ng_ = h // hpb
    wid = hpb * LANE

    def body(q_ref, k_ref, v_ref, tq_ref, tk_ref, tv_ref, z_ref, ba_ref, wq_ref, wk_ref, wv_ref, sc_ref, ng_ref,
             o_ref, s_ref, state):
        g, ti = pl.program_id(0), pl.program_id(1)

        @pl.when(ti == 0)
        def _():
            state[...] = jnp.zeros_like(state)

        live = (ti > 0).astype(F32)
        ba, sc_, gain = ba_ref[...], sc_ref[...], ng_ref[...]
        for hh in range(hpb):
            sl = slice(hh * LANE, (hh + 1) * LANE)
            s_ref[hh] = state[hh]
            o, new_state = _gdn_chunk(
                q_ref[:, sl], k_ref[:, sl], v_ref[:, sl], tq_ref[:, sl] * live, tk_ref[:, sl] * live, tv_ref[:, sl] * live,
                z_ref[:, sl], ba, wq_ref[:, sl], wk_ref[:, sl], wv_ref[:, sl], sc_, gain, state[hh], g * hpb + hh)
            o_ref[:, sl] = o.astype(o_ref.dtype)
            state[hh] = new_state

    def blk(off):
        return pl.BlockSpec((tb, wid), lambda g, ti: (ti, off + g))

    def tail(off):
        return pl.BlockSpec((SUB, wid), _tail_map(tb, lambda g: off + g))

    def wsp(off):
        return pl.BlockSpec((4, wid), lambda g, ti: (0, off + g))

    in_specs = [blk(0), blk(ng_), blk(2 * ng_), tail(0), tail(ng_), tail(2 * ng_), blk(3 * ng_),
                pl.BlockSpec((tb, LANE), lambda g, ti: (ti, 6 * h)), wsp(0), wsp(ng_), wsp(2 * ng_),
                pl.BlockSpec((SUB, LANE), lambda g, ti: (0, 0)), pl.BlockSpec((1, LANE), lambda g, ti: (0, 0))]
    out_specs = (pl.BlockSpec((tb, wid), lambda g, ti: (ti, g)),
                 pl.BlockSpec((hpb, None, LANE, LANE), lambda g, ti: (g, ti, 0, 0)))
    return pl.pallas_call(
        body, name="gdn_fwd", grid=(ng_, nb), in_specs=in_specs, out_specs=out_specs,
        out_shape=(jax.ShapeDtypeStruct((t, GDN_V), BF16), jax.ShapeDtypeStruct((h, nb, LANE, LANE), F32)),
        scratch_shapes=[pltpu.VMEM((hpb, LANE, LANE), F32)], compiler_params=_params(2))(
            p, p, p, p, p, p, p, p, conv_w, conv_w, conv_w, sc, ng)


def gdn_bwd(p, conv_w, sc, ng, states, dmix):
    t = p.shape[0]
    tb = GDN_CHUNK
    nb = t // tb
    h = GDN_HEADS
    hpb = GDN_HPB
    ng_ = h // hpb
    wid = hpb * LANE

    def body(q_ref, k_ref, v_ref, tq_ref, tk_ref, tv_ref, z_ref, ba_ref, wq_ref, wk_ref, wv_ref, sc_ref, ng_ref,
             s_ref, do_ref, dq_ref, dk_ref, dv_ref, dz_ref, dba_ref, dwq_ref, dwk_ref, dwv_ref, dsc_ref, dng_ref,
             dstate, cq, ck, cv):
        g, step = pl.program_id(0), pl.program_id(1)
        head = g
        ti = nb - 1 - step

        @pl.when(step == 0)
        def _():
            dstate[...] = jnp.zeros_like(dstate)
            cq[...] = jnp.zeros_like(cq)
            ck[...] = jnp.zeros_like(ck)
            cv[...] = jnp.zeros_like(cv)
            dwq_ref[...] = jnp.zeros_like(dwq_ref)
            dwk_ref[...] = jnp.zeros_like(dwk_ref)
            dwv_ref[...] = jnp.zeros_like(dwv_ref)

        @pl.when((step == 0) & (head == 0))
        def _():
            dsc_ref[...] = jnp.zeros_like(dsc_ref)
            dng_ref[...] = jnp.zeros_like(dng_ref)

        live = (ti > 0).astype(F32)
        ba, sc_, gain = ba_ref[...], sc_ref[...], ng_ref[...]
        pad = jnp.zeros((tb - SUB, LANE), F32)
        for hh in range(hpb):
            sl = slice(hh * LANE, (hh + 1) * LANE)
            fn = functools.partial(_gdn_chunk, head=g * hpb + hh)
            _, vjp = jax.vjp(
                fn, q_ref[:, sl], k_ref[:, sl], v_ref[:, sl], tq_ref[:, sl] * live, tk_ref[:, sl] * live,
                tv_ref[:, sl] * live, z_ref[:, sl], ba, wq_ref[:, sl], wk_ref[:, sl], wv_ref[:, sl], sc_, gain, s_ref[hh])
            (dq, dk, dv, dtq, dtk, dtv, dz, dba, dwq, dwk, dwv, dsc, dng, ds) = vjp((do_ref[:, sl], dstate[hh]))
            dq_ref[:, sl] = (dq + jnp.concatenate([pad, cq[:, sl]], axis=0)).astype(dq_ref.dtype)
            dk_ref[:, sl] = (dk + jnp.concatenate([pad, ck[:, sl]], axis=0)).astype(dk_ref.dtype)
            dv_ref[:, sl] = (dv + jnp.concatenate([pad, cv[:, sl]], axis=0)).astype(dv_ref.dtype)
            cq[:, sl] = dtq * live
            ck[:, sl] = dtk * live
            cv[:, sl] = dtv * live
            dz_ref[:, sl] = dz.astype(dz_ref.dtype)
            dba_ref[hh] = dba
            dwq_ref[:, sl] += dwq
            dwk_ref[:, sl] += dwk
            dwv_ref[:, sl] += dwv
            dsc_ref[...] += dsc
            dng_ref[...] += dng
            dstate[hh] = ds

    def blk(off):
        return pl.BlockSpec((tb, wid), lambda g, s: (nb - 1 - s, off + g))

    def tail(off):
        return pl.BlockSpec((SUB, wid), _tail_map_rev(tb, nb, lambda g: off + g))

    def wsp(off):
        return pl.BlockSpec((4, wid), lambda g, s: (0, off + g))

    shared8 = pl.BlockSpec((SUB, LANE), lambda g, s: (0, 0))
    shared1 = pl.BlockSpec((1, LANE), lambda g, s: (0, 0))
    in_specs = [blk(0), blk(ng_), blk(2 * ng_), tail(0), tail(ng_), tail(2 * ng_), blk(3 * ng_),
                pl.BlockSpec((tb, LANE), lambda g, s: (nb - 1 - s, 6 * h)), wsp(0), wsp(ng_), wsp(2 * ng_), shared8, shared1,
                pl.BlockSpec((hpb, None, LANE, LANE), lambda g, s: (g, nb - 1 - s, 0, 0)), blk(0)]
    oblk = pl.BlockSpec((tb, wid), lambda g, s: (nb - 1 - s, g))
    wacc = pl.BlockSpec((4, wid), lambda g, s: (0, g))
    out_specs = (oblk, oblk, oblk, oblk, pl.BlockSpec((hpb, tb, LANE), lambda g, s: (g, nb - 1 - s, 0)),
                 wacc, wacc, wacc, shared8, shared1)
    act = jax.ShapeDtypeStruct((t, GDN_V), BF16)
    wsh = jax.ShapeDtypeStruct((4, GDN_V), F32)
    out_shape = (act, act, act, act, jax.ShapeDtypeStruct((h, t, LANE), F32), wsh, wsh, wsh,
                 jax.ShapeDtypeStruct((SUB, LANE), F32), jax.ShapeDtypeStruct((1, LANE), F32))
    scratch = [pltpu.VMEM((hpb, LANE, LANE), F32)] + [pltpu.VMEM((SUB, wid), F32)] * 3
    return pl.pallas_call(
        body, name="gdn_bwd", grid=(ng_, nb), in_specs=in_specs, out_specs=out_specs, out_shape=out_shape,
        scratch_shapes=scratch, compiler_params=_params(2))(
            p, p, p, p, p, p, p, p, conv_w, conv_w, conv_w, sc, ng, states, dmix)


LRU_TB = 256
LRU_X_OFF = 32
LRU_G_OFF = 40


def lru_fwd(p, cw, cb, wr, br, wi, bi, lam):
    t = p.shape[0]
    tb = min(LRU_TB, t)
    nb = t // tb
    g8 = LRU_BLOCKS

    def body(x_ref, tx_ref, lg_ref, cw_ref, cb_ref, wr_ref, br_ref, wi_ref, bi_ref, lam_ref, o_ref, hs_ref, hstate):
        ti = pl.program_id(1)

        @pl.when(ti == 0)
        def _():
            hstate[...] = jnp.zeros_like(hstate)

        live = (ti > 0).astype(F32)
        hs_ref[...] = hstate[...]
        o, hlast = _lru_block(x_ref[...], tx_ref[...] * live, lg_ref[...], cw_ref[...], cb_ref[...], wr_ref[...],
                              br_ref[...], wi_ref[...], bi_ref[...], lam_ref[...], hstate[0:1, :])
        o_ref[...] = o.astype(o_ref.dtype)
        hstate[...] = jnp.broadcast_to(hlast, hstate.shape)

    vec = pl.BlockSpec((1, LANE), lambda g, ti: (0, g))
    mat = pl.BlockSpec((None, LANE, LANE), lambda g, ti: (g, 0, 0))
    in_specs = [pl.BlockSpec((tb, LANE), lambda g, ti: (ti, LRU_X_OFF + g)),
                pl.BlockSpec((SUB, LANE), _tail_map(tb, lambda g: LRU_X_OFF + g)),
                pl.BlockSpec((tb, LANE), lambda g, ti: (ti, LRU_G_OFF + g)),
                pl.BlockSpec((4, LANE), lambda g, ti: (0, g)), vec, mat, vec, mat, vec, vec]
    out_specs = (pl.BlockSpec((tb, LANE), lambda g, ti: (ti, g)),
                 pl.BlockSpec((None, None, SUB, LANE), lambda g, ti: (g, ti, 0, 0)))
    return pl.pallas_call(
        body, name="lru_fwd", grid=(g8, nb), in_specs=in_specs, out_specs=out_specs,
        out_shape=(jax.ShapeDtypeStruct((t, LRU_WIDTH), BF16), jax.ShapeDtypeStruct((g8, nb, SUB, LANE), F32)),
        scratch_shapes=[pltpu.VMEM((SUB, LANE), F32)], compiler_params=_params(2))(
            p, p, p, cw, cb, wr, br, wi, bi, lam)


def lru_bwd(p, cw, cb, wr, br, wi, bi, lam, hs, dmix):
    t = p.shape[0]
    tb = min(LRU_TB, t)
    nb = t // tb
    g8 = LRU_BLOCKS

    def body(x_ref, tx_ref, lg_ref, cw_ref, cb_ref, wr_ref, br_ref, wi_ref, bi_ref, lam_ref, hs_ref, do_ref,
             dx_ref, dlg_ref, dcw_ref, dcb_ref, dwr_ref, dbr_ref, dwi_ref, dbi_ref, dlam_ref, dh, cx):
        step = pl.program_id(1)
        ti = nb - 1 - step

        @pl.when(step == 0)
        def _():
            dh[...] = jnp.zeros_like(dh)
            cx[...] = jnp.zeros_like(cx)
            for r in (dcw_ref, dcb_ref, dwr_ref, dbr_ref, dwi_ref, dbi_ref, dlam_ref):
                r[...] = jnp.zeros_like(r)

        live = (ti > 0).astype(F32)
        _, vjp = jax.vjp(_lru_block, x_ref[...], tx_ref[...] * live, lg_ref[...], cw_ref[...], cb_ref[...], wr_ref[...],
                         br_ref[...], wi_ref[...], bi_ref[...], lam_ref[...], hs_ref[0:1, :])
        dx, dtx, dlg, dcw, dcb, dwr, dbr, dwi, dbi, dlam, dhp = vjp((do_ref[...], dh[0:1, :]))
        pad = jnp.zeros((tb - SUB, LANE), F32)
        dx_ref[...] = (dx + jnp.concatenate([pad, cx[...]], axis=0)).astype(dx_ref.dtype)
        cx[...] = dtx * live
        dlg_ref[...] = dlg.astype(dlg_ref.dtype)
        dcw_ref[...] += dcw
        dcb_ref[...] += dcb
        dwr_ref[...] += dwr
        dbr_ref[...] += dbr
        dwi_ref[...] += dwi
        dbi_ref[...] += dbi
        dlam_ref[...] += dlam
        dh[...] = jnp.broadcast_to(dhp, dh.shape)

    vec = pl.BlockSpec((1, LANE), lambda g, s: (0, g))
    mat = pl.BlockSpec((None, LANE, LANE), lambda g, s: (g, 0, 0))
    cwsp = pl.BlockSpec((4, LANE), lambda g, s: (0, g))
    in_specs = [pl.BlockSpec((tb, LANE), lambda g, s: (nb - 1 - s, LRU_X_OFF + g)),
                pl.BlockSpec((SUB, LANE), _tail_map_rev(tb, nb, lambda g: LRU_X_OFF + g)),
                pl.BlockSpec((tb, LANE), lambda g, s: (nb - 1 - s, LRU_G_OFF + g)),
                cwsp, vec, mat, vec, mat, vec, vec,
                pl.BlockSpec((None, None, SUB, LANE), lambda g, s: (g, nb - 1 - s, 0, 0)),
                pl.BlockSpec((tb, LANE), lambda g, s: (nb - 1 - s, LRU_BLOCKS + g))]
    oblk = pl.BlockSpec((tb, LANE), lambda g, s: (nb - 1 - s, g))
    out_specs = (oblk, oblk, cwsp, vec, mat, vec, mat, vec, vec)
    act = jax.ShapeDtypeStruct((t, LRU_WIDTH), BF16)
    vsh = jax.ShapeDtypeStruct((1, LRU_WIDTH), F32)
    msh = jax.ShapeDtypeStruct((g8, LANE, LANE), F32)
    out_shape = (act, act, jax.ShapeDtypeStruct((4, LRU_WIDTH), F32), vsh, msh, vsh, msh, vsh, vsh)
    return pl.pallas_call(
        body, name="lru_bwd", grid=(g8, nb), in_specs=in_specs, out_specs=out_specs, out_shape=out_shape,
        scratch_shapes=[pltpu.VMEM((SUB, LANE), F32), pltpu.VMEM((SUB, LANE), F32)], compiler_params=_params(2))(
            p, p, p, cw, cb, wr, br, wi, bi, lam, hs, dmix)


SWA_GROUPS = 4
SWA_QW = SWA_Q // SWA_GROUPS
SWA_K_OFF = SWA_Q // LANE
SWA_V_OFF = (SWA_Q + SWA_KV) // LANE


def rope_tables(pos):
    t = pos.shape[0]
    tb = _tile(t, (512, 256, 128))
    inv = 1.0 / (ROPE_THETA ** (jnp.arange(0, SWA_HD, 2, dtype=F32) / SWA_HD))
    inv = jnp.concatenate([inv, inv]).reshape(1, SWA_HD)

    def body(p_ref, i_ref, c_ref, s_ref):
        ang = p_ref[...] * i_ref[...]
        c_ref[...] = jnp.cos(ang)
        s_ref[...] = jnp.sin(ang)

    blk = pl.BlockSpec((tb, SWA_HD), lambda i: (i, 0))
    sh = jax.ShapeDtypeStruct((t, SWA_HD), F32)
    return pl.pallas_call(
        body, name="rope_tables", grid=(t // tb,),
        in_specs=[pl.BlockSpec((tb, 1), lambda i: (i, 0)), pl.BlockSpec((1, SWA_HD), lambda i: (0, 0))],
        out_specs=(blk, blk), out_shape=(sh, sh), compiler_params=_params(1))(pos, inv)


def _rot_matrix():
    r, c = _rows((SWA_HD, SWA_HD)), _lanes((SWA_HD, SWA_HD))
    half = SWA_HD // 2
    return jnp.where(r == c + half, -1.0, 0.0) + jnp.where(r + half == c, 1.0, 0.0)


def _swa_specs(tb, cur, prev):
    def q_sp():
        return pl.BlockSpec((tb, SWA_QW), lambda g, s: (cur(s), g))

    def kv(off, which):
        return pl.BlockSpec((tb, LANE), lambda g, s: (which(s), off + g))

    def tab(which):
        return pl.BlockSpec((tb, SWA_HD), lambda g, s: (which(s), 0))

    nrm = pl.BlockSpec((1, SWA_HD), lambda g, s: (0, 0))
    snk = pl.BlockSpec((1, LANE), lambda g, s: (0, 0))
    return [q_sp(), kv(SWA_K_OFF, cur), kv(SWA_K_OFF, prev), kv(SWA_V_OFF, cur), kv(SWA_V_OFF, prev),
            tab(cur), tab(cur), tab(prev), tab(prev), nrm, nrm, snk]


def swa_fwd(p, cos, sin, qn, kn, sinks):
    t = p.shape[0]
    tb = SWA_BLOCK
    nb = t // tb
    per = SWA_HEADS // SWA_KV_HEADS

    def body(q_ref, kc_ref, kp_ref, vc_ref, vp_ref, cc_ref, sc_ref, cp_ref, sp_ref, qn_ref, kn_ref, sk_ref, o_ref):
        g, ti = pl.program_id(0), pl.program_id(1)
        rot = _rot_matrix()
        for j in range(2):
            ksl = slice(j * SWA_HD, (j + 1) * SWA_HD)
            qs = [q_ref[:, (j * per + i) * SWA_HD:(j * per + i + 1) * SWA_HD] for i in range(per)]
            outs = _swa_kvhead(qs, kc_ref[:, ksl], kp_ref[:, ksl], vc_ref[:, ksl], vp_ref[:, ksl], cc_ref[...], sc_ref[...],
                               cp_ref[...], sp_ref[...], qn_ref[...], kn_ref[...], sk_ref[...],
                               g * (2 * per) + j * per, ti > 0, rot)
            for i in range(per):
                o_ref[:, (j * per + i) * SWA_HD:(j * per + i + 1) * SWA_HD] = outs[i].astype(o_ref.dtype)

    in_specs = _swa_specs(tb, lambda s: s, lambda s: jnp.maximum(s - 1, 0))
    return pl.pallas_call(
        body, name="swa_fwd", grid=(SWA_GROUPS, nb), in_specs=in_specs,
        out_specs=pl.BlockSpec((tb, SWA_QW), lambda g, s: (s, g)), out_shape=jax.ShapeDtypeStruct((t, SWA_Q), BF16),
        compiler_params=_params(2))(p, p, p, p, p, cos, sin, cos, sin, qn, kn, sinks)


def swa_bwd(p, cos, sin, qn, kn, sinks, do):
    t = p.shape[0]
    tb = SWA_BLOCK
    nb = t // tb
    per = SWA_HEADS // SWA_KV_HEADS

    def body(q_ref, kc_ref, kp_ref, vc_ref, vp_ref, cc_ref, sc_ref, cp_ref, sp_ref, qn_ref, kn_ref, sk_ref, do_ref,
             dq_ref, dk_ref, dv_ref, dqn_ref, dkn_ref, dsk_ref, ck, cv):
        g, step = pl.program_id(0), pl.program_id(1)
        ti = nb - 1 - step
        rot = _rot_matrix()

        @pl.when(step == 0)
        def _():
            ck[...] = jnp.zeros_like(ck)
            cv[...] = jnp.zeros_like(cv)

        @pl.when((step == 0) & (g == 0))
        def _():
            dqn_ref[...] = jnp.zeros_like(dqn_ref)
            dkn_ref[...] = jnp.zeros_like(dkn_ref)
            dsk_ref[...] = jnp.zeros_like(dsk_ref)

        for j in range(2):
            ksl = slice(j * SWA_HD, (j + 1) * SWA_HD)
            qs = [q_ref[:, (j * per + i) * SWA_HD:(j * per + i + 1) * SWA_HD] for i in range(per)]
            dos = tuple(do_ref[:, (j * per + i) * SWA_HD:(j * per + i + 1) * SWA_HD].astype(F32) for i in range(per))
            cc, sc, cp, sp = cc_ref[...], sc_ref[...], cp_ref[...], sp_ref[...]
            head0 = g * (2 * per) + j * per

            def fn(qs_, kc, kp, vc, vp, qn_, kn_, sk):
                return _swa_kvhead(qs_, kc, kp, vc, vp, cc, sc, cp, sp, qn_, kn_, sk, head0, ti > 0, rot)

            _, vjp = jax.vjp(fn, qs, kc_ref[:, ksl], kp_ref[:, ksl], vc_ref[:, ksl], vp_ref[:, ksl], qn_ref[...],
                             kn_ref[...], sk_ref[...])
            dqs, dkc, dkp, dvc, dvp, dqn, dkn, dsk = vjp(dos)
            for i in range(per):
                dq_ref[:, (j * per + i) * SWA_HD:(j * per + i + 1) * SWA_HD] = dqs[i].astype(dq_ref.dtype)
            dk_ref[:, ksl] = (dkc + ck[:, ksl]).astype(dk_ref.dtype)
            dv_ref[:, ksl] = (dvc + cv[:, ksl]).astype(dv_ref.dtype)
            ck[:, ksl] = dkp
            cv[:, ksl] = dvp
            dqn_ref[...] += dqn
            dkn_ref[...] += dkn
            dsk_ref[...] += dsk

    in_specs = _swa_specs(tb, lambda s: nb - 1 - s, lambda s: jnp.maximum(nb - 2 - s, 0))
    in_specs.append(pl.BlockSpec((tb, SWA_QW), lambda g, s: (nb - 1 - s, g)))
    kvo = pl.BlockSpec((tb, LANE), lambda g, s: (nb - 1 - s, g))
    nrm = pl.BlockSpec((1, SWA_HD), lambda g, s: (0, 0))
    out_specs = (pl.BlockSpec((tb, SWA_QW), lambda g, s: (nb - 1 - s, g)), kvo, kvo, nrm, nrm,
                 pl.BlockSpec((1, LANE), lambda g, s: (0, 0)))
    out_shape = (jax.ShapeDtypeStruct((t, SWA_Q), BF16), jax.ShapeDtypeStruct((t, SWA_KV), BF16),
                 jax.ShapeDtypeStruct((t, SWA_KV), BF16), jax.ShapeDtypeStruct((1, SWA_HD), F32),
                 jax.ShapeDtypeStruct((1, SWA_HD), F32), jax.ShapeDtypeStruct((1, LANE), F32))
    return pl.pallas_call(
        body, name="swa_bwd", grid=(SWA_GROUPS, nb), in_specs=in_specs, out_specs=out_specs, out_shape=out_shape,
        scratch_shapes=[pltpu.VMEM((tb, LANE), F32), pltpu.VMEM((tb, LANE), F32)], compiler_params=_params(2))(
            p, p, p, p, p, cos, sin, cos, sin, qn, kn, sinks, do)


def xattn_fwd(q, kv, qn, kn):
    t = q.shape[0]
    mlen = kv.shape[0]
    tb = _tile(t, (512, 256, 128, 64))

    def body(q_ref, k_ref, v_ref, qn_ref, kn_ref, o_ref):
        o_ref[...] = _xattn_head(q_ref[...], k_ref[...], v_ref[...], qn_ref[...], kn_ref[...]).astype(o_ref.dtype)

    nrm = pl.BlockSpec((1, X_HD), lambda g, s: (0, 0))
    in_specs = [pl.BlockSpec((tb, X_HD), lambda g, s: (s, g)), pl.BlockSpec((mlen, X_HD), lambda g, s: (0, g)),
                pl.BlockSpec((mlen, X_HD), lambda g, s: (0, X_HEADS + g)), nrm, nrm]
    return pl.pallas_call(
        body, name="xattn_fwd", grid=(X_HEADS, t // tb), in_specs=in_specs,
        out_specs=pl.BlockSpec((tb, X_HD), lambda g, s: (s, g)), out_shape=jax.ShapeDtypeStruct((t, X_INNER), BF16),
        compiler_params=_params(2))(q, kv, kv, qn, kn)


def xattn_bwd(q, kv, qn, kn, do):
    t = q.shape[0]
    mlen = kv.shape[0]
    tb = _tile(t, (512, 256, 128, 64))

    def body(q_ref, k_ref, v_ref, qn_ref, kn_ref, do_ref, dq_ref, dk_ref, dv_ref, dqn_ref, dkn_ref):
        g, s = pl.program_id(0), pl.program_id(1)

        @pl.when(s == 0)
        def _():
            dk_ref[...] = jnp.zeros_like(dk_ref)
            dv_ref[...] = jnp.zeros_like(dv_ref)

        @pl.when((s == 0) & (g == 0))
        def _():
            dqn_ref[...] = jnp.zeros_like(dqn_ref)
            dkn_ref[...] = jnp.zeros_like(dkn_ref)

        _, vjp = jax.vjp(_xattn_head, q_ref[...], k_ref[...], v_ref[...], qn_ref[...], kn_ref[...])
        dq, dk, dv, dqn, dkn = vjp(do_ref[...].astype(F32))
        dq_ref[...] = dq.astype(dq_ref.dtype)
        dk_ref[...] += dk
        dv_ref[...] += dv
        dqn_ref[...] += dqn
        dkn_ref[...] += dkn

    nrm = pl.BlockSpec((1, X_HD), lambda g, s: (0, 0))
    qblk = pl.BlockSpec((tb, X_HD), lambda g, s: (s, g))
    kblk = pl.BlockSpec((mlen, X_HD), lambda g, s: (0, g))
    in_specs = [qblk, kblk, pl.BlockSpec((mlen, X_HD), lambda g, s: (0, X_HEADS + g)), nrm, nrm, qblk]
    out_specs = (qblk, kblk, kblk, nrm, nrm)
    msh = jax.ShapeDtypeStruct((mlen, X_INNER), F32)
    nsh = jax.ShapeDtypeStruct((1, X_HD), F32)
    return pl.pallas_call(
        body, name="xattn_bwd", grid=(X_HEADS, t // tb), in_specs=in_specs, out_specs=out_specs,
        out_shape=(jax.ShapeDtypeStruct((t, X_INNER), BF16), msh, msh, nsh, nsh), compiler_params=_params(2))(
            q, kv, kv, qn, kn, do)


FFN_CW = 512
FFN_TB = 512


def ffn_fwd(gu, cw, cb):
    t, f2 = gu.shape
    f = f2 // 2
    cwid = _tile(f, (FFN_CW, 256, 128))
    ng = f // cwid
    tb = min(FFN_TB, t)

    def body(g_ref, tg_ref, u_ref, cw_ref, cb_ref, o_ref):
        live = (pl.program_id(1) > 0).astype(F32)
        o_ref[...] = _ffn_block(g_ref[...], tg_ref[...] * live, u_ref[...], cw_ref[...], cb_ref[...]).astype(o_ref.dtype)

    in_specs = [pl.BlockSpec((tb, cwid), lambda g, s: (s, g)), pl.BlockSpec((SUB, cwid), _tail_map(tb, lambda g: g)),
                pl.BlockSpec((tb, cwid), lambda g, s: (s, ng + g)), pl.BlockSpec((3, cwid), lambda g, s: (0, g)),
                pl.BlockSpec((1, cwid), lambda g, s: (0, g))]
    return pl.pallas_call(
        body, name="ffn_fwd", grid=(ng, t // tb), in_specs=in_specs, out_specs=pl.BlockSpec((tb, cwid), lambda g, s: (s, g)),
        out_shape=jax.ShapeDtypeStruct((t, f), BF16), compiler_params=_params(2))(gu, gu, gu, cw, cb)


def ffn_bwd(gu, cw, cb, da):
    t, f2 = gu.shape
    f = f2 // 2
    cwid = _tile(f, (FFN_CW, 256, 128))
    ng = f // cwid
    tb = min(FFN_TB, t)
    nb = t // tb

    def body(g_ref, tg_ref, u_ref, cw_ref, cb_ref, da_ref, dg_ref, du_ref, dcw_ref, dcb_ref, cg):
        step = pl.program_id(1)
        ti = nb - 1 - step

        @pl.when(step == 0)
        def _():
            cg[...] = jnp.zeros_like(cg)
            dcw_ref[...] = jnp.zeros_like(dcw_ref)
            dcb_ref[...] = jnp.zeros_like(dcb_ref)

        live = (ti > 0).astype(F32)
        _, vjp = jax.vjp(_ffn_block, g_ref[...], tg_ref[...] * live, u_ref[...], cw_ref[...], cb_ref[...])
        dg, dtg, du, dcw, dcb = vjp(da_ref[...].astype(F32))
        pad = jnp.zeros((tb - SUB, cwid), F32)
        dg_ref[...] = (dg + jnp.concatenate([pad, cg[...]], axis=0)).astype(dg_ref.dtype)
        cg[...] = dtg * live
        du_ref[...] = du.astype(du_ref.dtype)
        dcw_ref[...] += dcw
        dcb_ref[...] += dcb

    blk = pl.BlockSpec((tb, cwid), lambda g, s: (nb - 1 - s, g))
    wsp = pl.BlockSpec((3, cwid), lambda g, s: (0, g))
    bsp = pl.BlockSpec((1, cwid), lambda g, s: (0, g))
    in_specs = [blk, pl.BlockSpec((SUB, cwid), _tail_map_rev(tb, nb, lambda g: g)),
                pl.BlockSpec((tb, cwid), lambda g, s: (nb - 1 - s, ng + g)), wsp, bsp, blk]
    act = jax.ShapeDtypeStruct((t, f), BF16)
    return pl.pallas_call(
        body, name="ffn_bwd", grid=(ng, nb), in_specs=in_specs, out_specs=(blk, blk, wsp, bsp),
        out_shape=(act, act, jax.ShapeDtypeStruct((3, f), F32), jax.ShapeDtypeStruct((1, f), F32)),
        scratch_shapes=[pltpu.VMEM((SUB, cwid), F32)], compiler_params=_params(2))(gu, gu, gu, cw, cb, da)


def adamw(parts, w, m, v, name="adamw"):
    _, r, c = parts.shape
    tr = r if r % 16 else _tile(r, (256, 128, 64, 32, 16))
    tc = c if c % LANE else _tile(c, (256, 128) if r % 16 else (512, 256, 128))

    def body(p_ref, w_ref, m_ref, v_ref, g_ref, d_ref, nm_ref, nv_ref):
        g = p_ref[0].astype(F32)
        for s in range(1, N_DEV):
            g = g + p_ref[s].astype(F32)
        m_new = ADAM_B1 * m_ref[...] + (1.0 - ADAM_B1) * g
        v_new = ADAM_B2 * v_ref[...] + (1.0 - ADAM_B2) * jnp.square(g)
        m_hat = m_new / (1.0 - ADAM_B1 ** ADAM_STEP)
        v_hat = v_new / (1.0 - ADAM_B2 ** ADAM_STEP)
        g_ref[...] = g
        d_ref[...] = -ADAM_LR * (m_hat / (jnp.sqrt(v_hat) + ADAM_EPS) + ADAM_WD * w_ref[...])
        nm_ref[...] = m_new
        nv_ref[...] = v_new

    blk = pl.BlockSpec((tr, tc), lambda i, j: (i, j))
    sh = jax.ShapeDtypeStruct((r, c), F32)
    return pl.pallas_call(
        body, name=name, grid=(r // tr, c // tc),
        in_specs=[pl.BlockSpec((N_DEV, tr, tc), lambda i, j: (0, i, j)), blk, blk, blk],
        out_specs=(blk, blk, blk, blk), out_shape=(sh, sh, sh, sh), compiler_params=_params(2))(parts, w, m, v)


def _me():
    return lax.axis_index("x"), lax.axis_index("y"), lax.axis_index("c")


def _flip(coords, k):
    x, y, c = coords
    return (1 - x if k & 4 else x, 1 - y if k & 2 else y, 1 - c if k & 1 else c)


def _slot(coords):
    x, y, c = coords
    return 4 * x + 2 * y + c


def _comm_call(body, name, ins, out_shapes):
    n = len(ins)
    hbm = pl.BlockSpec(memory_space=pl.ANY)
    return pl.pallas_call(
        body, name=name, out_shape=tuple(out_shapes), in_specs=[hbm] * n, out_specs=tuple([hbm] * n),
        scratch_shapes=[pltpu.SemaphoreType.DMA((7 * n,)), pltpu.SemaphoreType.DMA((7 * n,)), pltpu.SemaphoreType.DMA((n,))],
    )(*ins)


def all_gather(shards, name):
    n = len(shards)

    def body(*refs):
        x_refs, out_refs = refs[:n], refs[n:2 * n]
        send_sems, recv_sems, local_sems = refs[2 * n:]
        me = _me()
        sibling = _flip(me, 1)
        chips = [2, 4, 6]

        def copy(i, k, block, to, own=False):
            dst = out_refs[i].at[_slot(block)]
            return pltpu.make_async_remote_copy(
                src_ref=x_refs[i] if own else dst, dst_ref=dst, send_sem=send_sems.at[7 * i + k],
                recv_sem=recv_sems.at[7 * i + k], device_id=to, device_id_type=pl.DeviceIdType.MESH)

        mine = [pltpu.make_async_copy(x_refs[i], out_refs[i].at[_slot(me)], local_sems.at[i]) for i in range(n)]
        for cp in mine:
            cp.start()
        first = []
        for j, k in enumerate(chips):
            first += [copy(i, 1 + j, me, _flip(me, k), own=True) for i in range(n)]
        first += [copy(i, 0, me, sibling, own=True) for i in range(n)]
        for cp in first:
            cp.start()
        passed = []
        for j, k in enumerate(chips):
            for i in range(n):
                copy(i, 1 + j, _flip(me, k), me).wait_recv()
                cp = copy(i, 4 + j, _flip(me, k), sibling)
                cp.start()
                passed.append(cp)
        for i in range(n):
            copy(i, 0, sibling, me).wait_recv()
        for j, k in enumerate(chips):
            for i in range(n):
                copy(i, 4 + j, _flip(sibling, k), me).wait_recv()
        for cp in first + passed:
            cp.wait_send()
        for cp in mine:
            cp.wait()

    return _comm_call(body, name, shards, [jax.ShapeDtypeStruct((N_DEV,) + s.shape, s.dtype) for s in shards])


def all_to_all(blocks, name):
    n = len(blocks)

    def body(*refs):
        x_refs, out_refs = refs[:n], refs[n:2 * n]
        send_sems, recv_sems, local_sems = refs[2 * n:]
        me = _me()
        my_slot = _slot(me)
        mine = [pltpu.make_async_copy(x_refs[i].at[my_slot], out_refs[i].at[my_slot], local_sems.at[i]) for i in range(n)]
        for cp in mine:
            cp.start()

        def copy(i, k, peer):
            return pltpu.make_async_remote_copy(
                src_ref=x_refs[i].at[_slot(peer)], dst_ref=out_refs[i].at[my_slot], send_sem=send_sems.at[7 * i + k - 1],
                recv_sem=recv_sems.at[7 * i + k - 1], device_id=peer, device_id_type=pl.DeviceIdType.MESH)

        order = [2, 4, 6, 3, 5, 7, 1]
        copies = [copy(i, k, _flip(me, k)) for k in order for i in range(n)]
        for cp in copies:
            cp.start()
        for k in order:
            peer = _flip(me, k)
            for i in range(n):
                pltpu.make_async_remote_copy(
                    src_ref=x_refs[i].at[my_slot], dst_ref=out_refs[i].at[_slot(peer)], send_sem=send_sems.at[7 * i + k - 1],
                    recv_sem=recv_sems.at[7 * i + k - 1], device_id=peer, device_id_type=pl.DeviceIdType.MESH).wait_recv()
        for cp in copies:
            cp.wait_send()
        for cp in mine:
            cp.wait()

    return _comm_call(body, name, blocks, [jax.ShapeDtypeStruct(b.shape, b.dtype) for b in blocks])


def _pack_rows(n):
    rows = -(-n // PACK_COLS)
    return -(-rows // 16) * 16


def _pack_flat(pieces, dtype):
    flat = jnp.concatenate([p.reshape(-1).astype(dtype) for p in pieces])
    rows = _pack_rows(flat.shape[0])
    return jnp.pad(flat, (0, rows * PACK_COLS - flat.shape[0])).reshape(rows, PACK_COLS)


def _unpack_flat(buf, shapes):
    lead = buf.shape[:-2]
    flat = buf.reshape(lead + (-1,))
    out, off = [], 0
    for sh in shapes:
        n = int(np.prod(sh))
        out.append(flat[..., off:off + n].reshape(lead + tuple(sh)))
        off += n
    return out


def _merge(gathered, axis):
    if axis == 0:
        return gathered.reshape(-1, gathered.shape[2])
    return jnp.transpose(gathered, (1, 0, 2)).reshape(gathered.shape[1], -1)


def _split(full, axis):
    a, b = full.shape
    if axis == 0:
        return full.reshape(N_DEV, a // N_DEV, b)
    return jnp.transpose(full.reshape(a, N_DEV, b // N_DEV), (1, 0, 2))


SHARDED = {
    "xq_w": 0, "xkv_w": 0, "xo_w": 1, "ffn_in_w": 1, "ffn_conv_w": 1, "ffn_out_w": 0,
    "hyb_in_w": 1, "hyb_out_w": 0, "gdn_conv_w": 1, "lru_conv_w": 1, "swa_in_w": 1, "swa_out_w": 0,
}
F32_SHIPPED = ("ffn_conv_w", "gdn_conv_w", "lru_conv_w")
TRANSPOSED = ("xo_w", "ffn_in_w", "hyb_in_w", "swa_in_w")
COMMON = ("xq_w", "xkv_w", "xo_w", "ffn_in_w", "ffn_conv_w", "ffn_out_w")
EVEN = ("hyb_in_w", "hyb_out_w", "gdn_conv_w", "lru_conv_w")
ODD = ("swa_in_w", "swa_out_w")
WEIGHTS = ["norm_mix", "norm_cross", "norm_mem", "norm_ffn", "xq_w", "xkv_w", "xo_w", "xq_norm", "xk_norm", "ffn_in_w",
           "ffn_conv_w", "ffn_conv_b", "ffn_out_w", "hyb_in_w", "hyb_out_w", "gdn_conv_w", "gdn_a_log", "gdn_dt_bias",
           "gdn_norm", "lru_conv_w", "lru_conv_b", "lru_wr", "lru_br", "lru_wi", "lru_bi", "lru_lambda", "swa_in_w",
           "swa_out_w", "swa_q_norm", "swa_k_norm", "swa_sinks"]
REPLICATED = [n for n in WEIGHTS if n not in SHARDED]


def _layer_names(l):
    return COMMON + (EVEN if l % 2 == 0 else ODD)


PER_LAYER = COMMON + ("norm_mix", "norm_cross", "norm_mem", "norm_ffn", "xq_norm", "xk_norm", "ffn_conv_b")


def _layer_index(name, l):
    return l if name in PER_LAYER else l // 2


def _working(name, shard):
    return shard.T if name in TRANSPOSED else shard


def _gather_layer(w, l):
    names = _layer_names(l)
    pieces = []
    for n in names:
        sh = _working(n, w[n][_layer_index(n, l)])
        pieces.append(sh if n in F32_SHIPPED else sh.astype(BF16))
    gathered = all_gather(pieces, name=f"gather_w{l % 2}")
    full = {}
    for n, g in zip(names, gathered):
        full[n] = _merge(g, 1) if n in F32_SHIPPED else g.reshape(-1, g.shape[2])
    return full


def _hyb_pad(wt):
    d = wt.shape[1]
    return jnp.concatenate([wt[:4096], wt[4112:HYB_IN], wt[4096:4112], jnp.zeros((HYB_PAD - HYB_IN, d), wt.dtype)], axis=0)


def _hyb_unpad(dwt):
    return jnp.concatenate([dwt[:4096], dwt[6144:6160], dwt[4096:6144]], axis=0)


def _pad_lanes(v, n=LANE):
    return jnp.pad(v.reshape(1, -1), ((0, 0), (0, n - v.shape[-1])))


def _layer_fwd(l, x, mem, full, w, cos, sin):
    e = l // 2
    sv = {"x": x}
    g_mix = w["norm_mix"][l:l + 1]
    h1 = norm_fwd(x, g_mix)
    sv["h1"] = h1
    if l % 2 == 0:
        w_in = _hyb_pad(full["hyb_in_w"])
        p = matmul(h1, w_in, "nt", name="mm_hyb_in")
        sc = jnp.concatenate([_pad_lanes(w["gdn_a_log"][e]), _pad_lanes(w["gdn_dt_bias"][e]), jnp.zeros((6, LANE), F32)], 0)
        ng = w["gdn_norm"][e:e + 1]
        oa, states = gdn_fwd(p, full["gdn_conv_w"], sc, ng)
        lru_args = (full["lru_conv_w"], w["lru_conv_b"][e:e + 1], w["lru_wr"][e], w["lru_br"][e:e + 1], w["lru_wi"][e],
                    w["lru_bi"][e:e + 1], w["lru_lambda"][e:e + 1])
        ob, hs = lru_fwd(p, *lru_args)
        w_out = full["hyb_out_w"]
        x1 = matmul(oa, w_out[:GDN_V], "nn", res=x, name="mm_hyb_out_a")
        x1 = matmul(ob, w_out[GDN_V:], "nn", res=x1, name="mm_hyb_out_b")
        sv.update(w_in=w_in, p=p, sc=sc, ng=ng, states=states, lru_args=lru_args, hs=hs, oa=oa, ob=ob)
    else:
        p = matmul(h1, full["swa_in_w"], "nt", name="mm_swa_in")
        swa_args = (cos, sin, w["swa_q_norm"][e:e + 1], w["swa_k_norm"][e:e + 1], _pad_lanes(w["swa_sinks"][e]))
        o = swa_fwd(p, *swa_args)
        x1 = matmul(o, full["swa_out_w"], "nn", res=x, name="mm_swa_out")
        sv.update(p=p, swa_args=swa_args, o=o)
    hc = norm_fwd(x1, w["norm_cross"][l:l + 1])
    memn = norm_fwd(mem, w["norm_mem"][l:l + 1], name="norm_mem_fwd")
    q = matmul(hc, full["xq_w"], "nn", name="mm_xq")
    kv = matmul(memn, full["xkv_w"], "nn", name="mm_xkv")
    qn, kn = w["xq_norm"][l:l + 1], w["xk_norm"][l:l + 1]
    ox = xattn_fwd(q, kv, qn, kn)
    x2 = matmul(ox, full["xo_w"], "nt", res=x1, name="mm_xo")
    hf = norm_fwd(x2, w["norm_ffn"][l:l + 1])
    gu = matmul(hf, full["ffn_in_w"], "nt", name="mm_ffn_in")
    cb = w["ffn_conv_b"][l:l + 1]
    a = ffn_fwd(gu, full["ffn_conv_w"], cb)
    x3 = matmul(a, full["ffn_out_w"], "nn", res=x2, name="mm_ffn_out")
    sv.update(x1=x1, hc=hc, memn=memn, q=q, kv=kv, qn=qn, kn=kn, ox=ox, x2=x2, hf=hf, gu=gu, cb=cb, a=a)
    return x3, sv


def _layer_bwd(l, dx3, mem, full, w, sv):
    e = l // 2
    gs, gr = {}, {}
    f = full["ffn_out_w"].shape[0]
    da = matmul(dx3, full["ffn_out_w"], "nt", name="mm_ffn_out_da")
    gs["ffn_out_w"] = matmul(sv["a"], dx3, "tn", out_dtype=BF16, name="mm_ffn_out_dw")
    dgate, dup, dcw, dcb = ffn_bwd(sv["gu"], full["ffn_conv_w"], sv["cb"], da)
    gs["ffn_conv_w"], gr["ffn_conv_b"] = dcw, dcb
    w_in = full["ffn_in_w"]
    dhf = matmul(dgate, w_in[:f], "nn", name="mm_ffn_in_dh_g")
    dhf = matmul(dup, w_in[f:], "nn", res=dhf, name="mm_ffn_in_dh_u")
    gs["ffn_in_w"] = jnp.concatenate([matmul(dgate, sv["hf"], "tn", out_dtype=BF16, name="mm_ffn_in_dw_g"),
                                      matmul(dup, sv["hf"], "tn", out_dtype=BF16, name="mm_ffn_in_dw_u")], axis=0)
    dx2, gr["norm_ffn"] = norm_bwd(sv["x2"], w["norm_ffn"][l:l + 1], dhf, dx3)
    dox = matmul(dx2, full["xo_w"], "nn", name="mm_xo_do")
    gs["xo_w"] = matmul(dx2, sv["ox"], "tn", out_dtype=BF16, name="mm_xo_dw")
    dq, dk, dv, gr["xq_norm"], gr["xk_norm"] = xattn_bwd(sv["q"], sv["kv"], sv["qn"], sv["kn"], dox)
    dkv = jnp.concatenate([dk, dv], axis=1)
    dhc = matmul(dq, full["xq_w"], "nt", name="mm_xq_dh")
    gs["xq_w"] = matmul(sv["hc"], dq, "tn", out_dtype=BF16, name="mm_xq_dw")
    dmemn = matmul(dkv, full["xkv_w"], "nt", name="mm_xkv_dh")
    gs["xkv_w"] = matmul(sv["memn"], dkv, "tn", out_dtype=BF16, name="mm_xkv_dw")
    gr["norm_mem"] = norm_bwd(mem, w["norm_mem"][l:l + 1], dmemn, name="norm_mem_bwd")
    dx1, gr["norm_cross"] = norm_bwd(sv["x1"], w["norm_cross"][l:l + 1], dhc, dx2)
    if l % 2 == 0:
        w_out = full["hyb_out_w"]
        dmix = matmul(dx1, w_out, "nt", name="mm_hyb_out_dm")
        gs["hyb_out_w"] = jnp.concatenate([matmul(sv["oa"], dx1, "tn", out_dtype=BF16, name="mm_hyb_out_dw_a"),
                                           matmul(sv["ob"], dx1, "tn", out_dtype=BF16, name="mm_hyb_out_dw_b")], axis=0)
        dq_, dk_, dv_, dz, dba, dwq, dwk, dwv, dsc, dng = gdn_bwd(
            sv["p"], full["gdn_conv_w"], sv["sc"], sv["ng"], sv["states"], dmix)
        dlx, dlg, dcw, dcb, dwr, dbr, dwi, dbi, dlam = lru_bwd(sv["p"], *sv["lru_args"], sv["hs"], dmix)
        dba = jnp.sum(dba, axis=0).astype(BF16)
        dp = jnp.concatenate([dq_, dk_, dv_, dz, dlx, dlg, dba], axis=1)
        dh1 = matmul(dp, sv["w_in"], "nn", name="mm_hyb_in_dh")
        gs["hyb_in_w"] = _hyb_unpad(matmul(dp, sv["h1"], "tn", out_dtype=BF16, name="mm_hyb_in_dw"))
        gs["gdn_conv_w"] = jnp.concatenate([dwq, dwk, dwv], axis=1)
        gs["lru_conv_w"] = dcw
        gr.update(gdn_a_log=dsc[0, :GDN_HEADS], gdn_dt_bias=dsc[1, :GDN_HEADS], gdn_norm=dng[0], lru_conv_b=dcb[0],
                  lru_wr=dwr, lru_br=dbr[0], lru_wi=dwi, lru_bi=dbi[0], lru_lambda=dlam[0])
    else:
        do = matmul(dx1, full["swa_out_w"], "nt", name="mm_swa_out_do")
        gs["swa_out_w"] = matmul(sv["o"], dx1, "tn", out_dtype=BF16, name="mm_swa_out_dw")
        dq_, dk_, dv_, dqn, dkn, dsk = swa_bwd(sv["p"], *sv["swa_args"], do)
        dp = jnp.concatenate([dq_, dk_, dv_], axis=1)
        dh1 = matmul(dp, full["swa_in_w"], "nn", name="mm_swa_in_dh")
        gs["swa_in_w"] = matmul(dp, sv["h1"], "tn", out_dtype=BF16, name="mm_swa_in_dw")
        gr.update(swa_q_norm=dqn[0], swa_k_norm=dkn[0], swa_sinks=dsk[0, :SWA_HEADS])
    dx, gr["norm_mix"] = norm_bwd(sv["x"], w["norm_mix"][l:l + 1], dh1, dx1)
    for n in ("norm_ffn", "norm_mem", "norm_cross", "norm_mix", "xq_norm", "xk_norm", "ffn_conv_b"):
        gr[n] = gr[n][0]
    return dx, gs, gr


def _update_layer(l, gs, w, m, v):
    names = _layer_names(l)
    blocks = []
    for n in names:
        g = gs[n]
        blocks.append(_split(g, 1) if n in F32_SHIPPED else g.reshape(N_DEV, g.shape[0] // N_DEV, g.shape[1]))
    parts = all_to_all(blocks, name=f"exchange_g{l % 2}")
    out = {}
    for n, p in zip(names, parts):
        shards = [_working(n, t[n][_layer_index(n, l)]) for t in (w, m, v)]
        quad = adamw(p, *shards, name=f"adamw_{n}")
        out[n] = tuple(_working(n, q) for q in quad)
    return out


def kernel(x, mem, positions, norm_mix, norm_cross, norm_mem, norm_ffn, xq_w, xkv_w, xo_w, xq_norm, xk_norm, ffn_in_w, ffn_conv_w, ffn_conv_b, ffn_out_w, hyb_in_w, hyb_out_w, gdn_conv_w, gdn_a_log, gdn_dt_bias, gdn_norm, lru_conv_w, lru_conv_b, lru_wr, lru_br, lru_wi, lru_bi, lru_lambda, swa_in_w, swa_out_w, swa_q_norm, swa_k_norm, swa_sinks, loss_target, m_norm_mix, m_norm_cross, m_norm_mem, m_norm_ffn, m_xq_w, m_xkv_w, m_xo_w, m_xq_norm, m_xk_norm, m_ffn_in_w, m_ffn_conv_w, m_ffn_conv_b, m_ffn_out_w, m_hyb_in_w, m_hyb_out_w, m_gdn_conv_w, m_gdn_a_log, m_gdn_dt_bias, m_gdn_norm, m_lru_conv_w, m_lru_conv_b, m_lru_wr, m_lru_br, m_lru_wi, m_lru_bi, m_lru_lambda, m_swa_in_w, m_swa_out_w, m_swa_q_norm, m_swa_k_norm, m_swa_sinks, v_norm_mix, v_norm_cross, v_norm_mem, v_norm_ffn, v_xq_w, v_xkv_w, v_xo_w, v_xq_norm, v_xk_norm, v_ffn_in_w, v_ffn_conv_w, v_ffn_conv_b, v_ffn_out_w, v_hyb_in_w, v_hyb_out_w, v_gdn_conv_w, v_gdn_a_log, v_gdn_dt_bias, v_gdn_norm, v_lru_conv_w, v_lru_conv_b, v_lru_wr, v_lru_br, v_lru_wi, v_lru_bi, v_lru_lambda, v_swa_in_w, v_swa_out_w, v_swa_q_norm, v_swa_k_norm, v_swa_sinks):
    args = locals()
    w = {n: args[n] for n in WEIGHTS}
    m = {n: args["m_" + n] for n in WEIGHTS}
    v = {n: args["v_" + n] for n in WEIGHTS}
    depth = norm_mix.shape[0]
    xs = x[0]
    mems = mem[0]
    cos, sin = rope_tables(positions[0].astype(F32).reshape(-1, 1))

    saved, fulls = [], []
    h = xs
    for l in range(depth):
        full = _gather_layer(w, l)
        h, sv = _layer_fwd(l, h, mems, full, w, cos, sin)
        saved.append(sv)
        fulls.append(full)
    dy, lpart = loss_grad(h, loss_target[0])
    loss = lax.psum(lpart[0, 0], ("x", "y", "c"))

    res = {}
    rep = {n: [None] * w[n].shape[0] for n in REPLICATED}
    dh = dy
    for l in reversed(range(depth)):
        dh, gs, gr = _layer_bwd(l, dh, mems, fulls[l], w, saved[l])
        for n, g in gr.items():
            rep[n][_layer_index(n, l)] = g
        for n, quad in _update_layer(l, gs, w, m, v).items():
            res.setdefault(n, [None] * w[n].shape[0])[_layer_index(n, l)] = quad
    grad_x = dh[None]

    rep_shapes = [w[n].shape for n in REPLICATED]
    local = _pack_flat([jnp.stack(rep[n]).reshape(w[n].shape) for n in REPLICATED], F32)
    parts = all_gather([local], name="gather_rep")[0]
    outs = adamw(parts, *[_pack_flat([t[n] for n in REPLICATED], F32) for t in (w, m, v)], name="adamw_rep")
    per = [_unpack_flat(o, rep_shapes) for o in outs]
    quads = {n: tuple(per[k][i] for k in range(4)) for i, n in enumerate(REPLICATED)}
    for n in SHARDED:
        quads[n] = tuple(jnp.stack([res[n][i][k] for i in range(w[n].shape[0])]) for k in range(4))
    return (loss, grad_x, *[quads[n][0] for n in WEIGHTS], *[quads[n][1] for n in WEIGHTS],
            *[quads[n][2] for n in WEIGHTS], *[quads[n][3] for n in WEIGHTS])
```

```python
import functools
import math

import jax
import jax.numpy as jnp
import numpy as np
from jax import lax
from jax.experimental import pallas as pl
from jax.experimental.pallas import tpu as pltpu

F32 = jnp.float32
BF16 = jnp.bfloat16
HIGHEST = lax.Precision.HIGHEST

EPS = 1e-6
N_DEV = 8
GDN_HEADS = 8
GDN_DK = 128
GDN_CHUNK = 64
GDN_QKV = 3072
GDN_V = 1024
LRU_WIDTH = 1024
LRU_BLOCKS = 8
LRU_C = 8.0
SWA_HEADS = 32
SWA_KV_HEADS = 8
SWA_HD = 64
SWA_BLOCK = 128
SWA_Q = SWA_HEADS * SWA_HD
SWA_KV = SWA_KV_HEADS * SWA_HD
ROPE_THETA = 10000.0
X_HEADS = 4
X_HD = 128
X_INNER = X_HEADS * X_HD
HYB_IN = 6160
HYB_PAD = 6272
LANE = 128
SUB = 8
ADAM_LR = 0.001
ADAM_B1 = 0.9
ADAM_B2 = 0.999
ADAM_EPS = 1e-08
ADAM_WD = 0.01
ADAM_STEP = 10
PACK_COLS = 1024
VMEM_LIMIT = 56 * 1024 * 1024


def _params(n_grid):
    return pltpu.CompilerParams(dimension_semantics=("arbitrary",) * n_grid, vmem_limit_bytes=VMEM_LIMIT)


def _dg(a, b, ca, cb, hi):
    dims = (((ca,), (cb,)), ((), ()))
    if hi:
        return lax.dot_general(a, b, dims, precision=HIGHEST, preferred_element_type=F32)
    return lax.dot_general(a.astype(BF16), b.astype(BF16), dims, preferred_element_type=F32)


@functools.partial(jax.custom_vjp, nondiff_argnums=(2, 3, 4))
def mm(a, b, ca, cb, hi):
    return _dg(a, b, ca, cb, hi)


def _mm_fwd(a, b, ca, cb, hi):
    return _dg(a, b, ca, cb, hi), (a, b)


def _mm_bwd(ca, cb, hi, res, g):
    a, b = res
    if ca == 1:
        da = _dg(g, b, 1, 1 - cb, hi)
    else:
        da = _dg(b, g, 1 - cb, 1, hi)
    if cb == 0:
        db = _dg(a, g, 1 - ca, 0, hi)
    else:
        db = _dg(g, a, 0, 1 - ca, hi)
    return da, db


mm.defvjp(_mm_fwd, _mm_bwd)


def mm_nn(a, b, hi=False):
    return mm(a, b, 1, 0, hi)


def mm_nt(a, b, hi=False):
    return mm(a, b, 1, 1, hi)


def mm_tn(a, b, hi=False):
    return mm(a, b, 0, 0, hi)


def _rows(shape):
    return lax.broadcasted_iota(jnp.int32, shape, 0)


def _lanes(shape):
    return lax.broadcasted_iota(jnp.int32, shape, 1)


@functools.partial(jax.custom_vjp, nondiff_argnums=(2,))
def shift_rows(x, tail, k):
    rx = pltpu.roll(x, k, 0)
    rt = pltpu.roll(tail, k, 0)
    first = jnp.where(_rows(tail.shape) < k, rt, rx[:SUB])
    return jnp.concatenate([first, rx[SUB:]], axis=0)


def _shift_rows_fwd(x, tail, k):
    return shift_rows(x, tail, k), None


def _shift_rows_bwd(k, _, g):
    n = g.shape[0]
    dx = jnp.where(_rows(g.shape) < n - k, pltpu.roll(g, n - k, 0), 0.0)
    g8 = g[:SUB]
    dtail = jnp.where(_rows(g8.shape) >= SUB - k, pltpu.roll(g8, SUB - k, 0), 0.0)
    return dx, dtail


shift_rows.defvjp(_shift_rows_fwd, _shift_rows_bwd)


@functools.partial(jax.custom_vjp, nondiff_argnums=(1, 2))
def shift_fill(x, k, fill):
    return jnp.where(_rows(x.shape) >= k, pltpu.roll(x, k, 0), fill)


def _shift_fill_fwd(x, k, fill):
    return shift_fill(x, k, fill), None


def _shift_fill_bwd(k, fill, _, g):
    n = g.shape[0]
    return (jnp.where(_rows(g.shape) < n - k, pltpu.roll(g, n - k, 0), 0.0),)


shift_fill.defvjp(_shift_fill_fwd, _shift_fill_bwd)


def _row_of(w, j):
    return jnp.sum(jnp.where(_rows(w.shape) == j, w, 0.0), axis=0, keepdims=True)


def _lane_col(x, j):
    return jnp.sum(jnp.where(_lanes(x.shape) == j, x, 0.0), axis=1, keepdims=True)


def _scalar_at(w, r, j):
    m = (_rows(w.shape) == r) & (_lanes(w.shape) == j)
    s = jnp.sum(jnp.where(m, w, 0.0), axis=1, keepdims=True)
    return jnp.sum(s, axis=0, keepdims=True)


def _col2row(col, n):
    eye = _rows((n, n)) == _lanes((n, n))
    return jnp.sum(jnp.where(eye, jnp.broadcast_to(col, (n, n)), 0.0), axis=0, keepdims=True)


def _rms(x, g):
    return x * lax.rsqrt(jnp.mean(x * x, axis=-1, keepdims=True) + EPS) * g


def _conv_core(x, tail, w, width):
    y = x * _row_of(w, width - 1)
    for k in range(1, width):
        y = y + shift_rows(x, tail, k) * _row_of(w, width - 1 - k)
    return y


def _bf16_round(a):
    return a.astype(BF16).astype(F32)


@functools.partial(jax.custom_vjp, nondiff_argnums=(3,))
def _conv(x, tail, w, width):
    return _conv_core(_bf16_round(x), _bf16_round(tail), _bf16_round(w), width)


def _conv_fwd(x, tail, w, width):
    xb, tb, wb = _bf16_round(x), _bf16_round(tail), _bf16_round(w)
    return _conv_core(xb, tb, wb, width), (xb, tb, wb)


def _conv_bwd(width, res, g):
    _, vjp = jax.vjp(functools.partial(_conv_core, width=width), *res)
    return vjp(_bf16_round(g))


_conv.defvjp(_conv_fwd, _conv_bwd)


def _neg_expm1(x):
    t = jnp.tanh(0.5 * x)
    return -2.0 * t / (1.0 - t)


@jax.custom_vjp
def inv_unit_lower_all(ls):
    n = ls[0].shape[0]
    eye = (_rows(ls[0].shape) == _lanes(ls[0].shape)).astype(F32)
    ts = [eye - l for l in ls]
    ps = list(ls)
    k = 2
    while k < n:
        ps = [_dg(p, p, 1, 0, True) for p in ps]
        ts = [_dg(t, eye + p, 1, 0, True) for t, p in zip(ts, ps)]
        k *= 2
    return ts


def _inv_all_fwd(ls):
    ts = inv_unit_lower_all(ls)
    return ts, ts


def _inv_all_bwd(ts, gs):
    xs = [_dg(g, t, 1, 1, True) for g, t in zip(gs, ts)]
    return ([-_dg(t, x, 0, 0, True) for t, x in zip(ts, xs)],)


inv_unit_lower_all.defvjp(_inv_all_fwd, _inv_all_bwd)


def _gdn_chunks(per_head, ba, sc, ng, head0):
    c = GDN_CHUNK
    n = len(per_head)
    hs = range(n)
    qr, kr, vr, tq, tk, tv, z, wq, wk, wv, state = [list(col) for col in zip(*per_head)]
    qc = [jax.nn.silu(_conv(qr[i], tq[i], wq[i], 4)) for i in hs]
    kc = [jax.nn.silu(_conv(kr[i], tk[i], wk[i], 4)) for i in hs]
    v = [jax.nn.silu(_conv(vr[i], tv[i], wv[i], 4)) for i in hs]
    q = [x * lax.rsqrt(jnp.sum(x * x, axis=-1, keepdims=True) + EPS) * (GDN_DK ** -0.5) for x in qc]
    k = [x * lax.rsqrt(jnp.sum(x * x, axis=-1, keepdims=True) + EPS) for x in kc]
    beta = [jax.nn.sigmoid(_lane_col(ba, head0 + i)) for i in hs]
    gl = [-jnp.exp(_scalar_at(sc, 0, head0 + i))
          * jax.nn.softplus(_lane_col(ba, head0 + i + GDN_HEADS) + _scalar_at(sc, 1, head0 + i)) for i in hs]
    causal = _rows((c, c)) >= _lanes((c, c))
    strict = _rows((c, c)) > _lanes((c, c))
    gc = [jnp.sum(jnp.where(causal, _col2row(g, c), 0.0), axis=1, keepdims=True) for g in gl]
    decay = [jnp.exp(jnp.where(causal, g - _col2row(g, c), -jnp.inf)) for g in gc]
    kb = [k[i] * beta[i] for i in hs]
    kk = [mm_nt(kb[i], k[i]) for i in hs]
    qk = [mm_nt(q[i], k[i]) for i in hs]
    tmat = inv_unit_lower_all([jnp.where(strict, kk[i] * decay[i], 0.0) for i in hs])
    eg = [jnp.exp(g) for g in gc]
    u = [mm_nn(tmat[i], v[i] * beta[i]) for i in hs]
    w = [mm_nn(tmat[i], kb[i] * eg[i]) for i in hs]
    attn = [jnp.where(causal, qk[i] * decay[i], 0.0) for i in hs]
    g_last = [jnp.sum(g, axis=0, keepdims=True) for g in gl]
    ws = [mm_nn(w[i], state[i]) for i in hs]
    qs = [mm_nn(q[i] * eg[i], state[i]) for i in hs]
    v_new = [u[i] - ws[i] for i in hs]
    av = [mm_nn(attn[i], v_new[i]) for i in hs]
    kv = [mm_tn(k[i] * jnp.exp(g_last[i] - gc[i]), v_new[i]) for i in hs]
    out = [_rms(qs[i] + av[i], ng) * jax.nn.silu(z[i]) for i in hs]
    new_state = [state[i] * jnp.exp(g_last[i]) + kv[i] for i in hs]
    return list(zip(out, new_state))


def _lru_block(lx, tail, lg, cw, cb, wr, br, wi, bi, lam, hprev):
    n = lx.shape[0]
    xc = _conv(lx, tail, cw, 4) + cb
    r = jax.nn.sigmoid(mm_nn(xc, wr) + br)
    i = jax.nn.sigmoid(mm_nn(xc, wi) + bi)
    log_a = -LRU_C * r * jax.nn.softplus(-lam)
    a = jnp.exp(log_a)
    b = jnp.sqrt(_neg_expm1(2.0 * log_a)) * (i * xc)
    k = 1
    while k < n:
        b = a * shift_fill(b, k, 0.0) + b
        a = a * shift_fill(a, k, 1.0)
        k *= 2
    h = b + a * hprev
    hlast = jnp.sum(jnp.where(_rows(h.shape) == n - 1, h, 0.0), axis=0, keepdims=True)
    return h * jax.nn.gelu(lg), hlast


def _signed_perm(x, rot, cb):
    hi = x.astype(BF16)
    r1 = x - hi.astype(F32)
    mid = r1.astype(BF16)
    lo = (r1 - mid.astype(F32)).astype(BF16)
    rb = rot.astype(BF16)
    dims = (((1,), (cb,)), ((), ()))
    dot = functools.partial(lax.dot_general, dimension_numbers=dims, preferred_element_type=F32)
    return (dot(hi, rb) + dot(mid, rb)) + dot(lo, rb)


@jax.custom_vjp
def _rotate_half(x, rot):
    return _signed_perm(x, rot, 0)


def _rotate_half_fwd(x, rot):
    return _signed_perm(x, rot, 0), rot


def _rotate_half_bwd(rot, g):
    return _signed_perm(g, rot, 1), jnp.zeros_like(rot)


_rotate_half.defvjp(_rotate_half_fwd, _rotate_half_bwd)


def _rope(x, cos, sin, rot):
    return x * cos + _rotate_half(x, rot) * sin


def _swa_heads(groups, cosc, sinc, cosp, sinp, qn, kn, sinks, head0, not_first, rot):
    n = groups[0][1].shape[0]
    first_off = jnp.where(not_first, 0, n)
    scale = SWA_HD ** -0.5
    qi = _rows((n, n))
    kj = _lanes((n, n))
    kc_r = [_rope(_rms(grp[1], kn), cosc, sinc, rot) for grp in groups]
    kp_r = [_rope(_rms(grp[2], kn), cosp, sinp, rot) for grp in groups]
    heads = [(j, i) for j, grp in enumerate(groups) for i in range(len(grp[0]))]
    per = len(groups[0][0])
    q_r = [_rope(_rms(groups[j][0][i], qn), cosc, sinc, rot) for j, i in heads]
    s_c = [mm_nt(q_r[h], kc_r[j]) for h, (j, i) in enumerate(heads)]
    s_p = [mm_nt(q_r[h], kp_r[j]) for h, (j, i) in enumerate(heads)]
    s_c = [jnp.where(kj <= qi, s * scale, -jnp.inf) for s in s_c]
    s_p = [jnp.where(kj > qi + first_off, s * scale, -jnp.inf) for s in s_p]
    sink = [_scalar_at(sinks, 0, head0 + j * per + i) for j, i in heads]
    m = [lax.stop_gradient(jnp.maximum(jnp.maximum(jnp.max(s_c[h], axis=1, keepdims=True),
                                                   jnp.max(s_p[h], axis=1, keepdims=True)), sink[h]))
         for h in range(len(heads))]
    pc = [jnp.exp(s_c[h] - m[h]) for h in range(len(heads))]
    pp = [jnp.exp(s_p[h] - m[h]) for h in range(len(heads))]
    denom = [jnp.sum(pc[h], axis=1, keepdims=True) + jnp.sum(pp[h], axis=1, keepdims=True) + jnp.exp(sink[h] - m[h])
             for h in range(len(heads))]
    oc = [mm_nn(pc[h] / denom[h], groups[j][3]) for h, (j, i) in enumerate(heads)]
    op = [mm_nn(pp[h] / denom[h], groups[j][4]) for h, (j, i) in enumerate(heads)]
    outs = [oc[h] + op[h] for h in range(len(heads))]
    return [tuple(outs[j * per:(j + 1) * per]) for j in range(len(groups))]


def _xattn_head(q, k, v, qn, kn):
    qh = _rms(q, qn)
    kh = _rms(k, kn)
    s = mm_nt(qh, kh) * (X_HD ** -0.5)
    m = lax.stop_gradient(jnp.max(s, axis=1, keepdims=True))
    p = jnp.exp(s - m)
    p = p / jnp.sum(p, axis=1, keepdims=True)
    return mm_nn(p, v)


def _ffn_block(gate, tail, up, cw, cb):
    gt = _conv(gate, tail, cw, 3) + cb
    return jax.nn.silu(gt) * up


def _tile(n, cands):
    for c in cands:
        if n % c == 0:
            return c
    return n


def matmul(a, b, mode, res=None, out_dtype=F32, name="mm"):
    if mode == "nn":
        (m, k), (_, n) = a.shape, b.shape
    elif mode == "nt":
        (m, k), (n, _) = a.shape, b.shape
    else:
        (k, m), (_, n) = a.shape, b.shape
    tm = _tile(m, (1024, 1408, 896, 512, 256, 128))
    tn = _tile(n, (1408, 1024, 896, 768, 640, 512, 384, 256, 128))
    tk = _tile(k, (512, 896, 256, 128))
    nk = k // tk
    ca, cb = {"nn": (1, 0), "nt": (1, 1), "tn": (0, 0)}[mode]
    a_spec = pl.BlockSpec((tk, tm), lambda i, j, kk: (kk, i)) if mode == "tn" else pl.BlockSpec((tm, tk), lambda i, j, kk: (i, kk))
    b_spec = pl.BlockSpec((tn, tk), lambda i, j, kk: (j, kk)) if mode == "nt" else pl.BlockSpec((tk, tn), lambda i, j, kk: (kk, j))
    o_spec = pl.BlockSpec((tm, tn), lambda i, j, kk: (i, j))
    has_res = res is not None

    def body(*refs):
        if has_res:
            a_ref, b_ref, r_ref, o_ref, acc = refs
        else:
            a_ref, b_ref, o_ref, acc = refs
        kk = pl.program_id(2)

        @pl.when(kk == 0)
        def _():
            acc[...] = jnp.zeros_like(acc)

        acc[...] += _dg(a_ref[...], b_ref[...], ca, cb, False)

        @pl.when(kk == nk - 1)
        def _():
            r = acc[...]
            if has_res:
                r = r + r_ref[...].astype(F32)
            o_ref[...] = r.astype(o_ref.dtype)

    ins = [a, b] + ([res] if has_res else [])
    specs = [a_spec, b_spec] + ([o_spec] if has_res else [])
    return pl.pallas_call(
        body, name=name, grid=(m // tm, n // tn, nk), in_specs=specs, out_specs=o_spec,
        out_shape=jax.ShapeDtypeStruct((m, n), out_dtype), scratch_shapes=[pltpu.VMEM((tm, tn), F32)],
        compiler_params=_params(3))(*ins)


def norm_fwd(x, g, name="norm_fwd"):
    t, d = x.shape
    tb = _tile(t, (512, 256, 128, 64))

    def body(x_ref, g_ref, o_ref):
        o_ref[...] = _rms(x_ref[...], g_ref[...]).astype(o_ref.dtype)

    return pl.pallas_call(
        body, name=name, grid=(t // tb,),
        in_specs=[pl.BlockSpec((tb, d), lambda i: (i, 0)), pl.BlockSpec((1, d), lambda i: (0, 0))],
        out_specs=pl.BlockSpec((tb, d), lambda i: (i, 0)), out_shape=jax.ShapeDtypeStruct((t, d), BF16),
        compiler_params=_params(1))(x, g)


def norm_bwd(x, g, dh, dres=None, name="norm_bwd"):
    t, d = x.shape
    tb = _tile(t, (256, 128, 64))
    with_dx = dres is not None

    def body(*refs):
        if with_dx:
            x_ref, g_ref, dh_ref, dr_ref, dx_ref, dg_ref = refs
        else:
            x_ref, g_ref, dh_ref, dg_ref = refs
        _, vjp = jax.vjp(_rms, x_ref[...], g_ref[...])
        dx, dg = vjp(dh_ref[...].astype(F32))

        @pl.when(pl.program_id(0) == 0)
        def _():
            dg_ref[...] = jnp.zeros_like(dg_ref)

        dg_ref[...] += dg
        if with_dx:
            dx_ref[...] = dx + dr_ref[...]

    blk = pl.BlockSpec((tb, d), lambda i: (i, 0))
    gsp = pl.BlockSpec((1, d), lambda i: (0, 0))
    if with_dx:
        return pl.pallas_call(
            body, name=name, grid=(t // tb,), in_specs=[blk, gsp, blk, blk], out_specs=(blk, gsp),
            out_shape=(jax.ShapeDtypeStruct((t, d), F32), jax.ShapeDtypeStruct((1, d), F32)),
            compiler_params=_params(1))(x, g, dh, dres)
    return pl.pallas_call(
        body, name=name, grid=(t // tb,), in_specs=[blk, gsp, blk], out_specs=gsp,
        out_shape=jax.ShapeDtypeStruct((1, d), F32), compiler_params=_params(1))(x, g, dh)


def loss_grad(y, target):
    t, d = y.shape
    tb = _tile(t, (512, 256, 128, 64))

    def body(y_ref, t_ref, dy_ref, l_ref):
        err = y_ref[...] - t_ref[...]
        dy_ref[...] = err / d

        @pl.when(pl.program_id(0) == 0)
        def _():
            l_ref[...] = jnp.zeros_like(l_ref)

        part = jnp.sum(jnp.mean(err * err, axis=-1, keepdims=True), axis=0, keepdims=True)
        l_ref[...] += 0.5 * part

    blk = pl.BlockSpec((tb, d), lambda i: (i, 0))
    lsp = pl.BlockSpec((SUB, LANE), lambda i: (0, 0))
    return pl.pallas_call(
        body, name="loss_grad", grid=(t // tb,), in_specs=[blk, blk], out_specs=(blk, lsp),
        out_shape=(jax.ShapeDtypeStruct((t, d), F32), jax.ShapeDtypeStruct((SUB, LANE), F32)),
        compiler_params=_params(1))(y, target)


def _tail_map(tb, col):
    return lambda g, t: (jnp.maximum(t * (tb // SUB) - 1, 0), col(g))


def _tail_map_rev(tb, nb, col):
    return lambda g, t: (jnp.maximum((nb - 1 - t) * (tb // SUB) - 1, 0), col(g))


GDN_HPB = 4

def gdn_fwd(p, conv_w, sc, ng):
    t = p.shape[0]
    tb = GDN_CHUNK
    nb = t // tb
    h = GDN_HEADS
    hpb = GDN_HPB
    ng_ = h // hpb
    wid = hpb * LANE

    def body(q_ref, k_ref, v_ref, tq_ref, tk_ref, tv_ref, z_ref, ba_ref, wq_ref, wk_ref, wv_ref, sc_ref, ng_ref,
             o_ref, s_ref, state):
        g, ti = pl.program_id(0), pl.program_id(1)

        @pl.when(ti == 0)
        def _():
            state[...] = jnp.zeros_like(state)

        live = (ti > 0).astype(F32)
        ba, sc_, gain = ba_ref[...], sc_ref[...], ng_ref[...]
        sls = [slice(hh * LANE, (hh + 1) * LANE) for hh in range(hpb)]
        ins = [(q_ref[:, sl], k_ref[:, sl], v_ref[:, sl], tq_ref[:, sl] * live, tk_ref[:, sl] * live, tv_ref[:, sl] * live,
                z_ref[:, sl], wq_ref[:, sl], wk_ref[:, sl], wv_ref[:, sl], state[hh]) for hh, sl in enumerate(sls)]
        outs = _gdn_chunks(ins, ba, sc_, gain, g * hpb)
        for hh, sl in enumerate(sls):
            s_ref[hh] = ins[hh][-1]
            o_ref[:, sl] = outs[hh][0].astype(o_ref.dtype)
            state[hh] = outs[hh][1]

    def blk(off):
        return pl.BlockSpec((tb, wid), lambda g, ti: (ti, off + g))

    def tail(off):
        return pl.BlockSpec((SUB, wid), _tail_map(tb, lambda g: off + g))

    def wsp(off):
        return pl.BlockSpec((4, wid), lambda g, ti: (0, off + g))

    in_specs = [blk(0), blk(ng_), blk(2 * ng_), tail(0), tail(ng_), tail(2 * ng_), blk(3 * ng_),
                pl.BlockSpec((tb, LANE), lambda g, ti: (ti, 6 * h)), wsp(0), wsp(ng_), wsp(2 * ng_),
                pl.BlockSpec((SUB, LANE), lambda g, ti: (0, 0)), pl.BlockSpec((1, LANE), lambda g, ti: (0, 0))]
    out_specs = (pl.BlockSpec((tb, wid), lambda g, ti: (ti, g)),
                 pl.BlockSpec((hpb, None, LANE, LANE), lambda g, ti: (g, ti, 0, 0)))
    return pl.pallas_call(
        body, name="gdn_fwd", grid=(ng_, nb), in_specs=in_specs, out_specs=out_specs,
        out_shape=(jax.ShapeDtypeStruct((t, GDN_V), BF16), jax.ShapeDtypeStruct((h, nb, LANE, LANE), F32)),
        scratch_shapes=[pltpu.VMEM((hpb, LANE, LANE), F32)], compiler_params=_params(2))(
            p, p, p, p, p, p, p, p, conv_w, conv_w, conv_w, sc, ng)


def gdn_bwd(p, conv_w, sc, ng, states, dmix):
    t = p.shape[0]
    tb = GDN_CHUNK
    nb = t // tb
    h = GDN_HEADS
    hpb = GDN_HPB
    ng_ = h // hpb
    wid = hpb * LANE

    def body(q_ref, k_ref, v_ref, tq_ref, tk_ref, tv_ref, z_ref, ba_ref, wq_ref, wk_ref, wv_ref, sc_ref, ng_ref,
             s_ref, do_ref, dq_ref, dk_ref, dv_ref, dz_ref, dba_ref, dwq_ref, dwk_ref, dwv_ref, dsc_ref, dng_ref,
             dstate, cq, ck, cv):
        g, step = pl.program_id(0), pl.program_id(1)
        head = g
        ti = nb - 1 - step

        @pl.when(step == 0)
        def _():
            dstate[...] = jnp.zeros_like(dstate)
            cq[...] = jnp.zeros_like(cq)
            ck[...] = jnp.zeros_like(ck)
            cv[...] = jnp.zeros_like(cv)
            dwq_ref[...] = jnp.zeros_like(dwq_ref)
            dwk_ref[...] = jnp.zeros_like(dwk_ref)
            dwv_ref[...] = jnp.zeros_like(dwv_ref)

        @pl.when((step == 0) & (head == 0))
        def _():
            dsc_ref[...] = jnp.zeros_like(dsc_ref)
            dng_ref[...] = jnp.zeros_like(dng_ref)

        live = (ti > 0).astype(F32)
        ba, sc_, gain = ba_ref[...], sc_ref[...], ng_ref[...]
        pad = jnp.zeros((tb - SUB, LANE), F32)
        sls = [slice(hh * LANE, (hh + 1) * LANE) for hh in range(hpb)]
        ins = [(q_ref[:, sl], k_ref[:, sl], v_ref[:, sl], tq_ref[:, sl] * live, tk_ref[:, sl] * live, tv_ref[:, sl] * live,
                z_ref[:, sl], wq_ref[:, sl], wk_ref[:, sl], wv_ref[:, sl], s_ref[hh]) for hh, sl in enumerate(sls)]
        cots = [(do_ref[:, sl], dstate[hh]) for hh, sl in enumerate(sls)]
        carry = [(cq[:, sl], ck[:, sl], cv[:, sl]) for sl in sls]
        _, vjp = jax.vjp(functools.partial(_gdn_chunks, head0=g * hpb), ins, ba, sc_, gain)
        grads, dba, dsc_sum, dng_sum = vjp(cots)
        dba_ref[...] = dba
        for hh, sl in enumerate(sls):
            (dq, dk, dv, dtq, dtk, dtv, dz, dwq, dwk, dwv, ds) = grads[hh]
            dq_ref[:, sl] = (dq + jnp.concatenate([pad, carry[hh][0]], axis=0)).astype(dq_ref.dtype)
            dk_ref[:, sl] = (dk + jnp.concatenate([pad, carry[hh][1]], axis=0)).astype(dk_ref.dtype)
            dv_ref[:, sl] = (dv + jnp.concatenate([pad, carry[hh][2]], axis=0)).astype(dv_ref.dtype)
            cq[:, sl] = dtq * live
            ck[:, sl] = dtk * live
            cv[:, sl] = dtv * live
            dz_ref[:, sl] = dz.astype(dz_ref.dtype)
            dwq_ref[:, sl] += dwq
            dwk_ref[:, sl] += dwk
            dwv_ref[:, sl] += dwv
            dstate[hh] = ds
        dsc_ref[...] += dsc_sum
        dng_ref[...] += dng_sum

    def blk(off):
        return pl.BlockSpec((tb, wid), lambda g, s: (nb - 1 - s, off + g))

    def tail(off):
        return pl.BlockSpec((SUB, wid), _tail_map_rev(tb, nb, lambda g: off + g))

    def wsp(off):
        return pl.BlockSpec((4, wid), lambda g, s: (0, off + g))

    shared8 = pl.BlockSpec((SUB, LANE), lambda g, s: (0, 0))
    shared1 = pl.BlockSpec((1, LANE), lambda g, s: (0, 0))
    in_specs = [blk(0), blk(ng_), blk(2 * ng_), tail(0), tail(ng_), tail(2 * ng_), blk(3 * ng_),
                pl.BlockSpec((tb, LANE), lambda g, s: (nb - 1 - s, 6 * h)), wsp(0), wsp(ng_), wsp(2 * ng_), shared8, shared1,
                pl.BlockSpec((hpb, None, LANE, LANE), lambda g, s: (g, nb - 1 - s, 0, 0)), blk(0)]
    oblk = pl.BlockSpec((tb, wid), lambda g, s: (nb - 1 - s, g))
    wacc = pl.BlockSpec((4, wid), lambda g, s: (0, g))
    out_specs = (oblk, oblk, oblk, oblk, pl.BlockSpec((None, tb, LANE), lambda g, s: (g, nb - 1 - s, 0)),
                 wacc, wacc, wacc, shared8, shared1)
    act = jax.ShapeDtypeStruct((t, GDN_V), BF16)
    wsh = jax.ShapeDtypeStruct((4, GDN_V), F32)
    out_shape = (act, act, act, act, jax.ShapeDtypeStruct((ng_, t, LANE), F32), wsh, wsh, wsh,
                 jax.ShapeDtypeStruct((SUB, LANE), F32), jax.ShapeDtypeStruct((1, LANE), F32))
    scratch = [pltpu.VMEM((hpb, LANE, LANE), F32)] + [pltpu.VMEM((SUB, wid), F32)] * 3
    return pl.pallas_call(
        body, name="gdn_bwd", grid=(ng_, nb), in_specs=in_specs, out_specs=out_specs, out_shape=out_shape,
        scratch_shapes=scratch, compiler_params=_params(2))(
            p, p, p, p, p, p, p, p, conv_w, conv_w, conv_w, sc, ng, states, dmix)


LRU_TB = 256
LRU_X_OFF = 32
LRU_G_OFF = 40


def lru_fwd(p, cw, cb, wr, br, wi, bi, lam):
    t = p.shape[0]
    tb = min(LRU_TB, t)
    nb = t // tb
    g8 = LRU_BLOCKS

    def body(x_ref, tx_ref, lg_ref, cw_ref, cb_ref, wr_ref, br_ref, wi_ref, bi_ref, lam_ref, o_ref, hs_ref, hstate):
        ti = pl.program_id(1)

        @pl.when(ti == 0)
        def _():
            hstate[...] = jnp.zeros_like(hstate)

        live = (ti > 0).astype(F32)
        hs_ref[...] = hstate[...]
        o, hlast = _lru_block(x_ref[...], tx_ref[...] * live, lg_ref[...], cw_ref[...], cb_ref[...], wr_ref[...],
                              br_ref[...], wi_ref[...], bi_ref[...], lam_ref[...], hstate[0:1, :])
        o_ref[...] = o.astype(o_ref.dtype)
        hstate[...] = jnp.broadcast_to(hlast, hstate.shape)

    vec = pl.BlockSpec((1, LANE), lambda g, ti: (0, g))
    mat = pl.BlockSpec((None, LANE, LANE), lambda g, ti: (g, 0, 0))
    in_specs = [pl.BlockSpec((tb, LANE), lambda g, ti: (ti, LRU_X_OFF + g)),
                pl.BlockSpec((SUB, LANE), _tail_map(tb, lambda g: LRU_X_OFF + g)),
                pl.BlockSpec((tb, LANE), lambda g, ti: (ti, LRU_G_OFF + g)),
                pl.BlockSpec((4, LANE), lambda g, ti: (0, g)), vec, mat, vec, mat, vec, vec]
    out_specs = (pl.BlockSpec((tb, LANE), lambda g, ti: (ti, g)),
                 pl.BlockSpec((None, None, SUB, LANE), lambda g, ti: (g, ti, 0, 0)))
    return pl.pallas_call(
        body, name="lru_fwd", grid=(g8, nb), in_specs=in_specs, out_specs=out_specs,
        out_shape=(jax.ShapeDtypeStruct((t, LRU_WIDTH), BF16), jax.ShapeDtypeStruct((g8, nb, SUB, LANE), F32)),
        scratch_shapes=[pltpu.VMEM((SUB, LANE), F32)], compiler_params=_params(2))(
            p, p, p, cw, cb, wr, br, wi, bi, lam)


def lru_bwd(p, cw, cb, wr, br, wi, bi, lam, hs, dmix):
    t = p.shape[0]
    tb = min(LRU_TB, t)
    nb = t // tb
    g8 = LRU_BLOCKS

    def body(x_ref, tx_ref, lg_ref, cw_ref, cb_ref, wr_ref, br_ref, wi_ref, bi_ref, lam_ref, hs_ref, do_ref,
             dx_ref, dlg_ref, dcw_ref, dcb_ref, dwr_ref, dbr_ref, dwi_ref, dbi_ref, dlam_ref, dh, cx):
        step = pl.program_id(1)
        ti = nb - 1 - step

        @pl.when(step == 0)
        def _():
            dh[...] = jnp.zeros_like(dh)
            cx[...] = jnp.zeros_like(cx)
            for r in (dcw_ref, dcb_ref, dwr_ref, dbr_ref, dwi_ref, dbi_ref, dlam_ref):
                r[...] = jnp.zeros_like(r)

        live = (ti > 0).astype(F32)
        _, vjp = jax.vjp(_lru_block, x_ref[...], tx_ref[...] * live, lg_ref[...], cw_ref[...], cb_ref[...], wr_ref[...],
                         br_ref[...], wi_ref[...], bi_ref[...], lam_ref[...], hs_ref[0:1, :])
        dx, dtx, dlg, dcw, dcb, dwr, dbr, dwi, dbi, dlam, dhp = vjp((do_ref[...], dh[0:1, :]))
        pad = jnp.zeros((tb - SUB, LANE), F32)
        dx_ref[...] = (dx + jnp.concatenate([pad, cx[...]], axis=0)).astype(dx_ref.dtype)
        cx[...] = dtx * live
        dlg_ref[...] = dlg.astype(dlg_ref.dtype)
        dcw_ref[...] += dcw
        dcb_ref[...] += dcb
        dwr_ref[...] += dwr
        dbr_ref[...] += dbr
        dwi_ref[...] += dwi
        dbi_ref[...] += dbi
        dlam_ref[...] += dlam
        dh[...] = jnp.broadcast_to(dhp, dh.shape)

    vec = pl.BlockSpec((1, LANE), lambda g, s: (0, g))
    mat = pl.BlockSpec((None, LANE, LANE), lambda g, s: (g, 0, 0))
    cwsp = pl.BlockSpec((4, LANE), lambda g, s: (0, g))
    in_specs = [pl.BlockSpec((tb, LANE), lambda g, s: (nb - 1 - s, LRU_X_OFF + g)),
                pl.BlockSpec((SUB, LANE), _tail_map_rev(tb, nb, lambda g: LRU_X_OFF + g)),
                pl.BlockSpec((tb, LANE), lambda g, s: (nb - 1 - s, LRU_G_OFF + g)),
                cwsp, vec, mat, vec, mat, vec, vec,
                pl.BlockSpec((None, None, SUB, LANE), lambda g, s: (g, nb - 1 - s, 0, 0)),
                pl.BlockSpec((tb, LANE), lambda g, s: (nb - 1 - s, LRU_BLOCKS + g))]
    oblk = pl.BlockSpec((tb, LANE), lambda g, s: (nb - 1 - s, g))
    out_specs = (oblk, oblk, cwsp, vec, mat, vec, mat, vec, vec)
    act = jax.ShapeDtypeStruct((t, LRU_WIDTH), BF16)
    vsh = jax.ShapeDtypeStruct((1, LRU_WIDTH), F32)
    msh = jax.ShapeDtypeStruct((g8, LANE, LANE), F32)
    out_shape = (act, act, jax.ShapeDtypeStruct((4, LRU_WIDTH), F32), vsh, msh, vsh, msh, vsh, vsh)
    return pl.pallas_call(
        body, name="lru_bwd", grid=(g8, nb), in_specs=in_specs, out_specs=out_specs, out_shape=out_shape,
        scratch_shapes=[pltpu.VMEM((SUB, LANE), F32), pltpu.VMEM((SUB, LANE), F32)], compiler_params=_params(2))(
            p, p, p, cw, cb, wr, br, wi, bi, lam, hs, dmix)


SWA_GROUPS = 4
SWA_QW = SWA_Q // SWA_GROUPS
SWA_K_OFF = SWA_Q // LANE
SWA_V_OFF = (SWA_Q + SWA_KV) // LANE


def rope_tables(pos):
    t = pos.shape[0]
    tb = _tile(t, (512, 256, 128))
    inv = 1.0 / (ROPE_THETA ** (jnp.arange(0, SWA_HD, 2, dtype=F32) / SWA_HD))
    inv = jnp.concatenate([inv, inv]).reshape(1, SWA_HD)

    def body(p_ref, i_ref, c_ref, s_ref):
        ang = p_ref[...] * i_ref[...]
        c_ref[...] = jnp.cos(ang)
        s_ref[...] = jnp.sin(ang)

    blk = pl.BlockSpec((tb, SWA_HD), lambda i: (i, 0))
    sh = jax.ShapeDtypeStruct((t, SWA_HD), F32)
    return pl.pallas_call(
        body, name="rope_tables", grid=(t // tb,),
        in_specs=[pl.BlockSpec((tb, 1), lambda i: (i, 0)), pl.BlockSpec((1, SWA_HD), lambda i: (0, 0))],
        out_specs=(blk, blk), out_shape=(sh, sh), compiler_params=_params(1))(pos, inv)


def _rot_matrix():
    r, c = _rows((SWA_HD, SWA_HD)), _lanes((SWA_HD, SWA_HD))
    half = SWA_HD // 2
    return jnp.where(r == c + half, -1.0, 0.0) + jnp.where(r + half == c, 1.0, 0.0)


def _swa_specs(tb, cur, prev):
    def q_sp():
        return pl.BlockSpec((tb, SWA_QW), lambda g, s: (cur(s), g))

    def kv(off, which):
        return pl.BlockSpec((tb, LANE), lambda g, s: (which(s), off + g))

    def tab(which):
        return pl.BlockSpec((tb, SWA_HD), lambda g, s: (which(s), 0))

    nrm = pl.BlockSpec((1, SWA_HD), lambda g, s: (0, 0))
    snk = pl.BlockSpec((1, LANE), lambda g, s: (0, 0))
    return [q_sp(), kv(SWA_K_OFF, cur), kv(SWA_K_OFF, prev), kv(SWA_V_OFF, cur), kv(SWA_V_OFF, prev),
            tab(cur), tab(cur), tab(prev), tab(prev), nrm, nrm, snk]


def swa_fwd(p, cos, sin, qn, kn, sinks):
    t = p.shape[0]
    tb = SWA_BLOCK
    nb = t // tb
    per = SWA_HEADS // SWA_KV_HEADS

    def body(q_ref, kc_ref, kp_ref, vc_ref, vp_ref, cc_ref, sc_ref, cp_ref, sp_ref, qn_ref, kn_ref, sk_ref, o_ref):
        g, ti = pl.program_id(0), pl.program_id(1)
        rot = _rot_matrix()
        tabs = (cc_ref[...], sc_ref[...], cp_ref[...], sp_ref[...], qn_ref[...], kn_ref[...], sk_ref[...])
        ksls = [slice(j * SWA_HD, (j + 1) * SWA_HD) for j in range(2)]
        ins = [([q_ref[:, (j * per + i) * SWA_HD:(j * per + i + 1) * SWA_HD] for i in range(per)],
                kc_ref[:, ksl], kp_ref[:, ksl], vc_ref[:, ksl], vp_ref[:, ksl]) for j, ksl in enumerate(ksls)]
        outs = _swa_heads(ins, *tabs, g * (2 * per), ti > 0, rot)
        for j in range(2):
            for i in range(per):
                o_ref[:, (j * per + i) * SWA_HD:(j * per + i + 1) * SWA_HD] = outs[j][i].astype(o_ref.dtype)

    in_specs = _swa_specs(tb, lambda s: s, lambda s: jnp.maximum(s - 1, 0))
    return pl.pallas_call(
        body, name="swa_fwd", grid=(SWA_GROUPS, nb), in_specs=in_specs,
        out_specs=pl.BlockSpec((tb, SWA_QW), lambda g, s: (s, g)), out_shape=jax.ShapeDtypeStruct((t, SWA_Q), BF16),
        compiler_params=_params(2))(p, p, p, p, p, cos, sin, cos, sin, qn, kn, sinks)


def swa_bwd(p, cos, sin, qn, kn, sinks, do):
    t = p.shape[0]
    tb = SWA_BLOCK
    nb = t // tb
    per = SWA_HEADS // SWA_KV_HEADS

    def body(q_ref, kc_ref, kp_ref, vc_ref, vp_ref, cc_ref, sc_ref, cp_ref, sp_ref, qn_ref, kn_ref, sk_ref, do_ref,
             dq_ref, dk_ref, dv_ref, dqn_ref, dkn_ref, dsk_ref, ck, cv):
        g, step = pl.program_id(0), pl.program_id(1)
        ti = nb - 1 - step
        rot = _rot_matrix()

        @pl.when(step == 0)
        def _():
            ck[...] = jnp.zeros_like(ck)
            cv[...] = jnp.zeros_like(cv)

        @pl.when((step == 0) & (g == 0))
        def _():
            dqn_ref[...] = jnp.zeros_like(dqn_ref)
            dkn_ref[...] = jnp.zeros_like(dkn_ref)
            dsk_ref[...] = jnp.zeros_like(dsk_ref)

        cc, sc, cp, sp = cc_ref[...], sc_ref[...], cp_ref[...], sp_ref[...]
        qn_v, kn_v, sk_v = qn_ref[...], kn_ref[...], sk_ref[...]
        ksls = [slice(j * SWA_HD, (j + 1) * SWA_HD) for j in range(2)]
        ins = [([q_ref[:, (j * per + i) * SWA_HD:(j * per + i + 1) * SWA_HD] for i in range(per)],
                kc_ref[:, ksl], kp_ref[:, ksl], vc_ref[:, ksl], vp_ref[:, ksl]) for j, ksl in enumerate(ksls)]
        dos = [tuple(do_ref[:, (j * per + i) * SWA_HD:(j * per + i + 1) * SWA_HD].astype(F32) for i in range(per))
               for j in range(2)]
        carry = [(ck[:, ksl], cv[:, ksl]) for ksl in ksls]
        def fn(groups, qn_, kn_, sk):
            return _swa_heads(groups, cc, sc, cp, sp, qn_, kn_, sk, g * (2 * per), ti > 0, rot)

        _, vjp = jax.vjp(fn, ins, qn_v, kn_v, sk_v)
        grads, dqn, dkn, dsk = vjp(dos)
        for j, ksl in enumerate(ksls):
            dqs, dkc, dkp, dvc, dvp = grads[j]
            for i in range(per):
                dq_ref[:, (j * per + i) * SWA_HD:(j * per + i + 1) * SWA_HD] = dqs[i].astype(dq_ref.dtype)
            dk_ref[:, ksl] = (dkc + carry[j][0]).astype(dk_ref.dtype)
            dv_ref[:, ksl] = (dvc + carry[j][1]).astype(dv_ref.dtype)
            ck[:, ksl] = dkp
            cv[:, ksl] = dvp
        dqn_ref[...] += dqn
        dkn_ref[...] += dkn
        dsk_ref[...] += dsk

    in_specs = _swa_specs(tb, lambda s: nb - 1 - s, lambda s: jnp.maximum(nb - 2 - s, 0))
    in_specs.append(pl.BlockSpec((tb, SWA_QW), lambda g, s: (nb - 1 - s, g)))
    kvo = pl.BlockSpec((tb, LANE), lambda g, s: (nb - 1 - s, g))
    nrm = pl.BlockSpec((1, SWA_HD), lambda g, s: (0, 0))
    out_specs = (pl.BlockSpec((tb, SWA_QW), lambda g, s: (nb - 1 - s, g)), kvo, kvo, nrm, nrm,
                 pl.BlockSpec((1, LANE), lambda g, s: (0, 0)))
    out_shape = (jax.ShapeDtypeStruct((t, SWA_Q), BF16), jax.ShapeDtypeStruct((t, SWA_KV), BF16),
                 jax.ShapeDtypeStruct((t, SWA_KV), BF16), jax.ShapeDtypeStruct((1, SWA_HD), F32),
                 jax.ShapeDtypeStruct((1, SWA_HD), F32), jax.ShapeDtypeStruct((1, LANE), F32))
    return pl.pallas_call(
        body, name="swa_bwd", grid=(SWA_GROUPS, nb), in_specs=in_specs, out_specs=out_specs, out_shape=out_shape,
        scratch_shapes=[pltpu.VMEM((tb, LANE), F32), pltpu.VMEM((tb, LANE), F32)], compiler_params=_params(2))(
            p, p, p, p, p, cos, sin, cos, sin, qn, kn, sinks, do)


def xattn_fwd(q, kv, qn, kn):
    t = q.shape[0]
    mlen = kv.shape[0]
    tb = _tile(t, (512, 256, 128, 64))

    def body(q_ref, k_ref, v_ref, qn_ref, kn_ref, o_ref):
        o_ref[...] = _xattn_head(q_ref[...], k_ref[...], v_ref[...], qn_ref[...], kn_ref[...]).astype(o_ref.dtype)

    nrm = pl.BlockSpec((1, X_HD), lambda g, s: (0, 0))
    in_specs = [pl.BlockSpec((tb, X_HD), lambda g, s: (s, g)), pl.BlockSpec((mlen, X_HD), lambda g, s: (0, g)),
                pl.BlockSpec((mlen, X_HD), lambda g, s: (0, X_HEADS + g)), nrm, nrm]
    return pl.pallas_call(
        body, name="xattn_fwd", grid=(X_HEADS, t // tb), in_specs=in_specs,
        out_specs=pl.BlockSpec((tb, X_HD), lambda g, s: (s, g)), out_shape=jax.ShapeDtypeStruct((t, X_INNER), BF16),
        compiler_params=_params(2))(q, kv, kv, qn, kn)


def xattn_bwd(q, kv, qn, kn, do):
    t = q.shape[0]
    mlen = kv.shape[0]
    tb = _tile(t, (512, 256, 128, 64))

    def body(q_ref, k_ref, v_ref, qn_ref, kn_ref, do_ref, dq_ref, dk_ref, dv_ref, dqn_ref, dkn_ref):
        g, s = pl.program_id(0), pl.program_id(1)

        @pl.when(s == 0)
        def _():
            dk_ref[...] = jnp.zeros_like(dk_ref)
            dv_ref[...] = jnp.zeros_like(dv_ref)

        @pl.when((s == 0) & (g == 0))
        def _():
            dqn_ref[...] = jnp.zeros_like(dqn_ref)
            dkn_ref[...] = jnp.zeros_like(dkn_ref)

        _, vjp = jax.vjp(_xattn_head, q_ref[...], k_ref[...], v_ref[...], qn_ref[...], kn_ref[...])
        dq, dk, dv, dqn, dkn = vjp(do_ref[...].astype(F32))
        dq_ref[...] = dq.astype(dq_ref.dtype)
        dk_ref[...] += dk
        dv_ref[...] += dv
        dqn_ref[...] += dqn
        dkn_ref[...] += dkn

    nrm = pl.BlockSpec((1, X_HD), lambda g, s: (0, 0))
    qblk = pl.BlockSpec((tb, X_HD), lambda g, s: (s, g))
    kblk = pl.BlockSpec((mlen, X_HD), lambda g, s: (0, g))
    in_specs = [qblk, kblk, pl.BlockSpec((mlen, X_HD), lambda g, s: (0, X_HEADS + g)), nrm, nrm, qblk]
    out_specs = (qblk, kblk, kblk, nrm, nrm)
    msh = jax.ShapeDtypeStruct((mlen, X_INNER), F32)
    nsh = jax.ShapeDtypeStruct((1, X_HD), F32)
    return pl.pallas_call(
        body, name="xattn_bwd", grid=(X_HEADS, t // tb), in_specs=in_specs, out_specs=out_specs,
        out_shape=(jax.ShapeDtypeStruct((t, X_INNER), BF16), msh, msh, nsh, nsh), compiler_params=_params(2))(
            q, kv, kv, qn, kn, do)


FFN_CW = 512
FFN_TB = 512


def ffn_fwd(gu, cw, cb):
    t, f2 = gu.shape
    f = f2 // 2
    cwid = _tile(f, (FFN_CW, 256, 128))
    ng = f // cwid
    tb = min(FFN_TB, t)

    def body(g_ref, tg_ref, u_ref, cw_ref, cb_ref, o_ref):
        live = (pl.program_id(1) > 0).astype(F32)
        o_ref[...] = _ffn_block(g_ref[...], tg_ref[...] * live, u_ref[...], cw_ref[...], cb_ref[...]).astype(o_ref.dtype)

    in_specs = [pl.BlockSpec((tb, cwid), lambda g, s: (s, g)), pl.BlockSpec((SUB, cwid), _tail_map(tb, lambda g: g)),
                pl.BlockSpec((tb, cwid), lambda g, s: (s, ng + g)), pl.BlockSpec((3, cwid), lambda g, s: (0, g)),
                pl.BlockSpec((1, cwid), lambda g, s: (0, g))]
    return pl.pallas_call(
        body, name="ffn_fwd", grid=(ng, t // tb), in_specs=in_specs, out_specs=pl.BlockSpec((tb, cwid), lambda g, s: (s, g)),
        out_shape=jax.ShapeDtypeStruct((t, f), BF16), compiler_params=_params(2))(gu, gu, gu, cw, cb)


def ffn_bwd(gu, cw, cb, da):
    t, f2 = gu.shape
    f = f2 // 2
    cwid = _tile(f, (FFN_CW, 256, 128))
    ng = f // cwid
    tb = min(FFN_TB, t)
    nb = t // tb

    def body(g_ref, tg_ref, u_ref, cw_ref, cb_ref, da_ref, dg_ref, du_ref, dcw_ref, dcb_ref, cg):
        step = pl.program_id(1)
        ti = nb - 1 - step

        @pl.when(step == 0)
        def _():
            cg[...] = jnp.zeros_like(cg)
            dcw_ref[...] = jnp.zeros_like(dcw_ref)
            dcb_ref[...] = jnp.zeros_like(dcb_ref)

        live = (ti > 0).astype(F32)
        _, vjp = jax.vjp(_ffn_block, g_ref[...], tg_ref[...] * live, u_ref[...], cw_ref[...], cb_ref[...])
        dg, dtg, du, dcw, dcb = vjp(da_ref[...].astype(F32))
        pad = jnp.zeros((tb - SUB, cwid), F32)
        dg_ref[...] = (dg + jnp.concatenate([pad, cg[...]], axis=0)).astype(dg_ref.dtype)
        cg[...] = dtg * live
        du_ref[...] = du.astype(du_ref.dtype)
        dcw_ref[...] += dcw
        dcb_ref[...] += dcb

    blk = pl.BlockSpec((tb, cwid), lambda g, s: (nb - 1 - s, g))
    wsp = pl.BlockSpec((3, cwid), lambda g, s: (0, g))
    bsp = pl.BlockSpec((1, cwid), lambda g, s: (0, g))
    in_specs = [blk, pl.BlockSpec((SUB, cwid), _tail_map_rev(tb, nb, lambda g: g)),
                pl.BlockSpec((tb, cwid), lambda g, s: (nb - 1 - s, ng + g)), wsp, bsp, blk]
    act = jax.ShapeDtypeStruct((t, f), BF16)
    return pl.pallas_call(
        body, name="ffn_bwd", grid=(ng, nb), in_specs=in_specs, out_specs=(blk, blk, wsp, bsp),
        out_shape=(act, act, jax.ShapeDtypeStruct((3, f), F32), jax.ShapeDtypeStruct((1, f), F32)),
        scratch_shapes=[pltpu.VMEM((SUB, cwid), F32)], compiler_params=_params(2))(gu, gu, gu, cw, cb, da)


def adamw(parts, w, m, v, name="adamw"):
    _, r, c = parts.shape
    tr = r if r % 16 else _tile(r, (256, 128, 64, 32, 16))
    tc = c if c % LANE else _tile(c, (256, 128) if r % 16 else (512, 256, 128))

    def body(p_ref, w_ref, m_ref, v_ref, g_ref, d_ref, nm_ref, nv_ref):
        g = p_ref[0].astype(F32)
        for s in range(1, N_DEV):
            g = g + p_ref[s].astype(F32)
        m_new = ADAM_B1 * m_ref[...] + (1.0 - ADAM_B1) * g
        v_new = ADAM_B2 * v_ref[...] + (1.0 - ADAM_B2) * jnp.square(g)
        m_hat = m_new / (1.0 - ADAM_B1 ** ADAM_STEP)
        v_hat = v_new / (1.0 - ADAM_B2 ** ADAM_STEP)
        g_ref[...] = g
        d_ref[...] = -ADAM_LR * (m_hat / (jnp.sqrt(v_hat) + ADAM_EPS) + ADAM_WD * w_ref[...])
        nm_ref[...] = m_new
        nv_ref[...] = v_new

    blk = pl.BlockSpec((tr, tc), lambda i, j: (i, j))
    sh = jax.ShapeDtypeStruct((r, c), F32)
    return pl.pallas_call(
        body, name=name, grid=(r // tr, c // tc),
        in_specs=[pl.BlockSpec((N_DEV, tr, tc), lambda i, j: (0, i, j)), blk, blk, blk],
        out_specs=(blk, blk, blk, blk), out_shape=(sh, sh, sh, sh), compiler_params=_params(2))(parts, w, m, v)


def _me():
    return lax.axis_index("x"), lax.axis_index("y"), lax.axis_index("c")


def _flip(coords, k):
    x, y, c = coords
    return (1 - x if k & 4 else x, 1 - y if k & 2 else y, 1 - c if k & 1 else c)


def _slot(coords):
    x, y, c = coords
    return 4 * x + 2 * y + c


def _comm_call(body, name, ins, out_shapes):
    n = len(ins)
    hbm = pl.BlockSpec(memory_space=pl.ANY)
    return pl.pallas_call(
        body, name=name, out_shape=tuple(out_shapes), in_specs=[hbm] * n, out_specs=tuple([hbm] * n),
        scratch_shapes=[pltpu.SemaphoreType.DMA((7 * n,)), pltpu.SemaphoreType.DMA((7 * n,)), pltpu.SemaphoreType.DMA((n,))],
    )(*ins)


def all_gather(shards, name):
    n = len(shards)

    def body(*refs):
        x_refs, out_refs = refs[:n], refs[n:2 * n]
        send_sems, recv_sems, local_sems = refs[2 * n:]
        me = _me()
        sibling = _flip(me, 1)
        chips = [2, 4, 6]

        def copy(i, k, block, to, own=False):
            dst = out_refs[i].at[_slot(block)]
            return pltpu.make_async_remote_copy(
                src_ref=x_refs[i] if own else dst, dst_ref=dst, send_sem=send_sems.at[7 * i + k],
                recv_sem=recv_sems.at[7 * i + k], device_id=to, device_id_type=pl.DeviceIdType.MESH)

        mine = [pltpu.make_async_copy(x_refs[i], out_refs[i].at[_slot(me)], local_sems.at[i]) for i in range(n)]
        for cp in mine:
            cp.start()
        first = []
        for j, k in enumerate(chips):
            first += [copy(i, 1 + j, me, _flip(me, k), own=True) for i in range(n)]
        first += [copy(i, 0, me, sibling, own=True) for i in range(n)]
        for cp in first:
            cp.start()
        passed = []
        for j, k in enumerate(chips):
            for i in range(n):
                copy(i, 1 + j, _flip(me, k), me).wait_recv()
                cp = copy(i, 4 + j, _flip(me, k), sibling)
                cp.start()
                passed.append(cp)
        for i in range(n):
            copy(i, 0, sibling, me).wait_recv()
        for j, k in enumerate(chips):
            for i in range(n):
                copy(i, 4 + j, _flip(sibling, k), me).wait_recv()
        for cp in first + passed:
            cp.wait_send()
        for cp in mine:
            cp.wait()

    return _comm_call(body, name, shards, [jax.ShapeDtypeStruct((N_DEV,) + s.shape, s.dtype) for s in shards])


def all_to_all(blocks, name):
    n = len(blocks)

    def body(*refs):
        x_refs, out_refs = refs[:n], refs[n:2 * n]
        send_sems, recv_sems, local_sems = refs[2 * n:]
        me = _me()
        my_slot = _slot(me)
        mine = [pltpu.make_async_copy(x_refs[i].at[my_slot], out_refs[i].at[my_slot], local_sems.at[i]) for i in range(n)]
        for cp in mine:
            cp.start()

        def copy(i, k, peer):
            return pltpu.make_async_remote_copy(
                src_ref=x_refs[i].at[_slot(peer)], dst_ref=out_refs[i].at[my_slot], send_sem=send_sems.at[7 * i + k - 1],
                recv_sem=recv_sems.at[7 * i + k - 1], device_id=peer, device_id_type=pl.DeviceIdType.MESH)

        order = [2, 4, 6, 3, 5, 7, 1]
        copies = [copy(i, k, _flip(me, k)) for k in order for i in range(n)]
        for cp in copies:
            cp.start()
        for k in order:
            peer = _flip(me, k)
            for i in range(n):
                pltpu.make_async_remote_copy(
                    src_ref=x_refs[i].at[my_slot], dst_ref=out_refs[i].at[_slot(peer)], send_sem=send_sems.at[7 * i + k - 1],
                    recv_sem=recv_sems.at[7 * i + k - 1], device_id=peer, device_id_type=pl.DeviceIdType.MESH).wait_recv()
        for cp in copies:
            cp.wait_send()
        for cp in mine:
            cp.wait()

    return _comm_call(body, name, blocks, [jax.ShapeDtypeStruct(b.shape, b.dtype) for b in blocks])


def _pack_rows(n):
    rows = -(-n // PACK_COLS)
    return -(-rows // 16) * 16


def _pack_flat(pieces, dtype):
    flat = jnp.concatenate([p.reshape(-1).astype(dtype) for p in pieces])
    rows = _pack_rows(flat.shape[0])
    return jnp.pad(flat, (0, rows * PACK_COLS - flat.shape[0])).reshape(rows, PACK_COLS)


def _unpack_flat(buf, shapes):
    lead = buf.shape[:-2]
    flat = buf.reshape(lead + (-1,))
    out, off = [], 0
    for sh in shapes:
        n = int(np.prod(sh))
        out.append(flat[..., off:off + n].reshape(lead + tuple(sh)))
        off += n
    return out


def _merge(gathered, axis):
    if axis == 0:
        return gathered.reshape(-1, gathered.shape[2])
    return jnp.transpose(gathered, (1, 0, 2)).reshape(gathered.shape[1], -1)


def _split(full, axis):
    a, b = full.shape
    if axis == 0:
        return full.reshape(N_DEV, a // N_DEV, b)
    return jnp.transpose(full.reshape(a, N_DEV, b // N_DEV), (1, 0, 2))


SHARDED = {
    "xq_w": 0, "xkv_w": 0, "xo_w": 1, "ffn_in_w": 1, "ffn_conv_w": 1, "ffn_out_w": 0,
    "hyb_in_w": 1, "hyb_out_w": 0, "gdn_conv_w": 1, "lru_conv_w": 1, "swa_in_w": 1, "swa_out_w": 0,
}
F32_SHIPPED = ("ffn_conv_w", "gdn_conv_w", "lru_conv_w")
TRANSPOSED = ("xo_w", "ffn_in_w", "hyb_in_w", "swa_in_w")
COMMON = ("xq_w", "xkv_w", "xo_w", "ffn_in_w", "ffn_conv_w", "ffn_out_w")
EVEN = ("hyb_in_w", "hyb_out_w", "gdn_conv_w", "lru_conv_w")
ODD = ("swa_in_w", "swa_out_w")
WEIGHTS = ["norm_mix", "norm_cross", "norm_mem", "norm_ffn", "xq_w", "xkv_w", "xo_w", "xq_norm", "xk_norm", "ffn_in_w",
           "ffn_conv_w", "ffn_conv_b", "ffn_out_w", "hyb_in_w", "hyb_out_w", "gdn_conv_w", "gdn_a_log", "gdn_dt_bias",
           "gdn_norm", "lru_conv_w", "lru_conv_b", "lru_wr", "lru_br", "lru_wi", "lru_bi", "lru_lambda", "swa_in_w",
           "swa_out_w", "swa_q_norm", "swa_k_norm", "swa_sinks"]
REPLICATED = [n for n in WEIGHTS if n not in SHARDED]


def _layer_names(l):
    return COMMON + (EVEN if l % 2 == 0 else ODD)


PER_LAYER = COMMON + ("norm_mix", "norm_cross", "norm_mem", "norm_ffn", "xq_norm", "xk_norm", "ffn_conv_b")


def _layer_index(name, l):
    return l if name in PER_LAYER else l // 2


def _working(name, shard):
    return shard.T if name in TRANSPOSED else shard


def _gather_layer(w, l):
    names = _layer_names(l)
    pieces = []
    for n in names:
        sh = _working(n, w[n][_layer_index(n, l)])
        pieces.append(sh if n in F32_SHIPPED else sh.astype(BF16))
    gathered = all_gather(pieces, name=f"gather_w{l % 2}")
    full = {}
    for n, g in zip(names, gathered):
        full[n] = _merge(g, 1) if n in F32_SHIPPED else g.reshape(-1, g.shape[2])
    return full


def _hyb_pad(wt):
    d = wt.shape[1]
    return jnp.concatenate([wt[:4096], wt[4112:HYB_IN], wt[4096:4112], jnp.zeros((HYB_PAD - HYB_IN, d), wt.dtype)], axis=0)


def _hyb_unpad(dwt):
    return jnp.concatenate([dwt[:4096], dwt[6144:6160], dwt[4096:6144]], axis=0)


def _pad_lanes(v, n=LANE):
    return jnp.pad(v.reshape(1, -1), ((0, 0), (0, n - v.shape[-1])))


def _layer_fwd(l, x, mem, full, w, cos, sin):
    e = l // 2
    sv = {"x": x}
    g_mix = w["norm_mix"][l:l + 1]
    h1 = norm_fwd(x, g_mix)
    sv["h1"] = h1
    if l % 2 == 0:
        w_in = _hyb_pad(full["hyb_in_w"])
        p = matmul(h1, w_in, "nt", name="mm_hyb_in")
        sc = jnp.concatenate([_pad_lanes(w["gdn_a_log"][e]), _pad_lanes(w["gdn_dt_bias"][e]), jnp.zeros((6, LANE), F32)], 0)
        ng = w["gdn_norm"][e:e + 1]
        oa, states = gdn_fwd(p, full["gdn_conv_w"], sc, ng)
        lru_args = (full["lru_conv_w"], w["lru_conv_b"][e:e + 1], w["lru_wr"][e], w["lru_br"][e:e + 1], w["lru_wi"][e],
                    w["lru_bi"][e:e + 1], w["lru_lambda"][e:e + 1])
        ob, hs = lru_fwd(p, *lru_args)
        w_out = full["hyb_out_w"]
        x1 = matmul(oa, w_out[:GDN_V], "nn", res=x, name="mm_hyb_out_a")
        x1 = matmul(ob, w_out[GDN_V:], "nn", res=x1, name="mm_hyb_out_b")
        sv.update(w_in=w_in, p=p, sc=sc, ng=ng, states=states, lru_args=lru_args, hs=hs, oa=oa, ob=ob)
    else:
        p = matmul(h1, full["swa_in_w"], "nt", name="mm_swa_in")
        swa_args = (cos, sin, w["swa_q_norm"][e:e + 1], w["swa_k_norm"][e:e + 1], _pad_lanes(w["swa_sinks"][e]))
        o = swa_fwd(p, *swa_args)
        x1 = matmul(o, full["swa_out_w"], "nn", res=x, name="mm_swa_out")
        sv.update(p=p, swa_args=swa_args, o=o)
    hc = norm_fwd(x1, w["norm_cross"][l:l + 1])
    memn = norm_fwd(mem, w["norm_mem"][l:l + 1], name="norm_mem_fwd")
    q = matmul(hc, full["xq_w"], "nn", name="mm_xq")
    kv = matmul(memn, full["xkv_w"], "nn", name="mm_xkv")
    qn, kn = w["xq_norm"][l:l + 1], w["xk_norm"][l:l + 1]
    ox = xattn_fwd(q, kv, qn, kn)
    x2 = matmul(ox, full["xo_w"], "nt", res=x1, name="mm_xo")
    hf = norm_fwd(x2, w["norm_ffn"][l:l + 1])
    gu = matmul(hf, full["ffn_in_w"], "nt", name="mm_ffn_in")
    cb = w["ffn_conv_b"][l:l + 1]
    a = ffn_fwd(gu, full["ffn_conv_w"], cb)
    x3 = matmul(a, full["ffn_out_w"], "nn", res=x2, name="mm_ffn_out")
    sv.update(x1=x1, hc=hc, memn=memn, q=q, kv=kv, qn=qn, kn=kn, ox=ox, x2=x2, hf=hf, gu=gu, cb=cb, a=a)
    return x3, sv


def _layer_bwd(l, dx3, mem, full, w, sv):
    e = l // 2
    gs, gr = {}, {}
    f = full["ffn_out_w"].shape[0]
    da = matmul(dx3, full["ffn_out_w"], "nt", name="mm_ffn_out_da")
    gs["ffn_out_w"] = matmul(sv["a"], dx3, "tn", out_dtype=BF16, name="mm_ffn_out_dw")
    dgate, dup, dcw, dcb = ffn_bwd(sv["gu"], full["ffn_conv_w"], sv["cb"], da)
    gs["ffn_conv_w"], gr["ffn_conv_b"] = dcw, dcb
    w_in = full["ffn_in_w"]
    dhf = matmul(dgate, w_in[:f], "nn", name="mm_ffn_in_dh_g")
    dhf = matmul(dup, w_in[f:], "nn", res=dhf, name="mm_ffn_in_dh_u")
    gs["ffn_in_w"] = jnp.concatenate([matmul(dgate, sv["hf"], "tn", out_dtype=BF16, name="mm_ffn_in_dw_g"),
                                      matmul(dup, sv["hf"], "tn", out_dtype=BF16, name="mm_ffn_in_dw_u")], axis=0)
    dx2, gr["norm_ffn"] = norm_bwd(sv["x2"], w["norm_ffn"][l:l + 1], dhf, dx3)
    dox = matmul(dx2, full["xo_w"], "nn", name="mm_xo_do")
    gs["xo_w"] = matmul(dx2, sv["ox"], "tn", out_dtype=BF16, name="mm_xo_dw")
    dq, dk, dv, gr["xq_norm"], gr["xk_norm"] = xattn_bwd(sv["q"], sv["kv"], sv["qn"], sv["kn"], dox)
    dkv = jnp.concatenate([dk, dv], axis=1)
    dhc = matmul(dq, full["xq_w"], "nt", name="mm_xq_dh")
    gs["xq_w"] = matmul(sv["hc"], dq, "tn", out_dtype=BF16, name="mm_xq_dw")
    dmemn = matmul(dkv, full["xkv_w"], "nt", name="mm_xkv_dh")
    gs["xkv_w"] = matmul(sv["memn"], dkv, "tn", out_dtype=BF16, name="mm_xkv_dw")
    gr["norm_mem"] = norm_bwd(mem, w["norm_mem"][l:l + 1], dmemn, name="norm_mem_bwd")
    dx1, gr["norm_cross"] = norm_bwd(sv["x1"], w["norm_cross"][l:l + 1], dhc, dx2)
    if l % 2 == 0:
        w_out = full["hyb_out_w"]
        dmix = matmul(dx1, w_out, "nt", name="mm_hyb_out_dm")
        gs["hyb_out_w"] = jnp.concatenate([matmul(sv["oa"], dx1, "tn", out_dtype=BF16, name="mm_hyb_out_dw_a"),
                                           matmul(sv["ob"], dx1, "tn", out_dtype=BF16, name="mm_hyb_out_dw_b")], axis=0)
        dq_, dk_, dv_, dz, dba, dwq, dwk, dwv, dsc, dng = gdn_bwd(
            sv["p"], full["gdn_conv_w"], sv["sc"], sv["ng"], sv["states"], dmix)
        dlx, dlg, dcw, dcb, dwr, dbr, dwi, dbi, dlam = lru_bwd(sv["p"], *sv["lru_args"], sv["hs"], dmix)
        dba = jnp.sum(dba, axis=0).astype(BF16)
        dp = jnp.concatenate([dq_, dk_, dv_, dz, dlx, dlg, dba], axis=1)
        dh1 = matmul(dp, sv["w_in"], "nn", name="mm_hyb_in_dh")
        gs["hyb_in_w"] = _hyb_unpad(matmul(dp, sv["h1"], "tn", out_dtype=BF16, name="mm_hyb_in_dw"))
        gs["gdn_conv_w"] = jnp.concatenate([dwq, dwk, dwv], axis=1)
        gs["lru_conv_w"] = dcw
        gr.update(gdn_a_log=dsc[0, :GDN_HEADS], gdn_dt_bias=dsc[1, :GDN_HEADS], gdn_norm=dng[0], lru_conv_b=dcb[0],
                  lru_wr=dwr, lru_br=dbr[0], lru_wi=dwi, lru_bi=dbi[0], lru_lambda=dlam[0])
    else:
        do = matmul(dx1, full["swa_out_w"], "nt", name="mm_swa_out_do")
        gs["swa_out_w"] = matmul(sv["o"], dx1, "tn", out_dtype=BF16, name="mm_swa_out_dw")
        dq_, dk_, dv_, dqn, dkn, dsk = swa_bwd(sv["p"], *sv["swa_args"], do)
        dp = jnp.concatenate([dq_, dk_, dv_], axis=1)
        dh1 = matmul(dp, full["swa_in_w"], "nn", name="mm_swa_in_dh")
        gs["swa_in_w"] = matmul(dp, sv["h1"], "tn", out_dtype=BF16, name="mm_swa_in_dw")
        gr.update(swa_q_norm=dqn[0], swa_k_norm=dkn[0], swa_sinks=dsk[0, :SWA_HEADS])
    dx, gr["norm_mix"] = norm_bwd(sv["x"], w["norm_mix"][l:l + 1], dh1, dx1)
    for n in ("norm_ffn", "norm_mem", "norm_cross", "norm_mix", "xq_norm", "xk_norm", "ffn_conv_b"):
        gr[n] = gr[n][0]
    return dx, gs, gr


def _update_layer(l, gs, w, m, v):
    names = _layer_names(l)
    blocks = []
    for n in names:
        g = gs[n]
        blocks.append(_split(g, 1) if n in F32_SHIPPED else g.reshape(N_DEV, g.shape[0] // N_DEV, g.shape[1]))
    parts = all_to_all(blocks, name=f"exchange_g{l % 2}")
    out = {}
    for n, p in zip(names, parts):
        shards = [_working(n, t[n][_layer_index(n, l)]) for t in (w, m, v)]
        quad = adamw(p, *shards, name=f"adamw_{n}")
        out[n] = tuple(_working(n, q) for q in quad)
    return out


def kernel(x, mem, positions, norm_mix, norm_cross, norm_mem, norm_ffn, xq_w, xkv_w, xo_w, xq_norm, xk_norm, ffn_in_w, ffn_conv_w, ffn_conv_b, ffn_out_w, hyb_in_w, hyb_out_w, gdn_conv_w, gdn_a_log, gdn_dt_bias, gdn_norm, lru_conv_w, lru_conv_b, lru_wr, lru_br, lru_wi, lru_bi, lru_lambda, swa_in_w, swa_out_w, swa_q_norm, swa_k_norm, swa_sinks, loss_target, m_norm_mix, m_norm_cross, m_norm_mem, m_norm_ffn, m_xq_w, m_xkv_w, m_xo_w, m_xq_norm, m_xk_norm, m_ffn_in_w, m_ffn_conv_w, m_ffn_conv_b, m_ffn_out_w, m_hyb_in_w, m_hyb_out_w, m_gdn_conv_w, m_gdn_a_log, m_gdn_dt_bias, m_gdn_norm, m_lru_conv_w, m_lru_conv_b, m_lru_wr, m_lru_br, m_lru_wi, m_lru_bi, m_lru_lambda, m_swa_in_w, m_swa_out_w, m_swa_q_norm, m_swa_k_norm, m_swa_sinks, v_norm_mix, v_norm_cross, v_norm_mem, v_norm_ffn, v_xq_w, v_xkv_w, v_xo_w, v_xq_norm, v_xk_norm, v_ffn_in_w, v_ffn_conv_w, v_ffn_conv_b, v_ffn_out_w, v_hyb_in_w, v_hyb_out_w, v_gdn_conv_w, v_gdn_a_log, v_gdn_dt_bias, v_gdn_norm, v_lru_conv_w, v_lru_conv_b, v_lru_wr, v_lru_br, v_lru_wi, v_lru_bi, v_lru_lambda, v_swa_in_w, v_swa_out_w, v_swa_q_norm, v_swa_k_norm, v_swa_sinks):
    args = locals()
    w = {n: args[n] for n in WEIGHTS}
    m = {n: args["m_" + n] for n in WEIGHTS}
    v = {n: args["v_" + n] for n in WEIGHTS}
    depth = norm_mix.shape[0]
    xs = x[0]
    mems = mem[0]
    cos, sin = rope_tables(positions[0].astype(F32).reshape(-1, 1))

    saved, fulls = [], []
    h = xs
    for l in range(depth):
        full = _gather_layer(w, l)
        h, sv = _layer_fwd(l, h, mems, full, w, cos, sin)
        saved.append(sv)
        fulls.append(full)
    dy, lpart = loss_grad(h, loss_target[0])
    loss = lax.psum(lpart[0, 0], ("x", "y", "c"))

    res = {}
    rep = {n: [None] * w[n].shape[0] for n in REPLICATED}
    dh = dy
    for l in reversed(range(depth)):
        dh, gs, gr = _layer_bwd(l, dh, mems, fulls[l], w, saved[l])
        for n, g in gr.items():
            rep[n][_layer_index(n, l)] = g
        for n, quad in _update_layer(l, gs, w, m, v).items():
            res.setdefault(n, [None] * w[n].shape[0])[_layer_index(n, l)] = quad
    grad_x = dh[None]

    rep_shapes = [w[n].shape for n in REPLICATED]
    local = _pack_flat([jnp.stack(rep[n]).reshape(w[n].shape) for n in REPLICATED], F32)
    parts = all_gather([local], name="gather_rep")[0]
    outs = adamw(parts, *[_pack_flat([t[n] for n in REPLICATED], F32) for t in (w, m, v)], name="adamw_rep")
    per = [_unpack_flat(o, rep_shapes) for o in outs]
    quads = {n: tuple(per[k][i] for k in range(4)) for i, n in enumerate(REPLICATED)}
    for n in SHARDED:
        quads[n] = tuple(jnp.stack([res[n][i][k] for i in range(w[n].shape[0])]) for k in range(4))
    return (loss, grad_x, *[quads[n][0] for n in WEIGHTS], *[quads[n][1] for n in WEIGHTS],
            *[quads[n][2] for n in WEIGHTS], *[quads[n][3] for n in WEIGHTS])
```

```python
import functools
import math

import jax
import jax.numpy as jnp
import numpy as np
from jax import lax
from jax.experimental import pallas as pl
from jax.experimental.pallas import tpu as pltpu

F32 = jnp.float32
BF16 = jnp.bfloat16
HIGHEST = lax.Precision.HIGHEST

EPS = 1e-6
N_DEV = 8
GDN_HEADS = 8
GDN_DK = 128
GDN_CHUNK = 64
GDN_QKV = 3072
GDN_V = 1024
LRU_WIDTH = 1024
LRU_BLOCKS = 8
LRU_C = 8.0
SWA_HEADS = 32
SWA_KV_HEADS = 8
SWA_HD = 64
SWA_BLOCK = 128
SWA_Q = SWA_HEADS * SWA_HD
SWA_KV = SWA_KV_HEADS * SWA_HD
ROPE_THETA = 10000.0
X_HEADS = 4
X_HD = 128
X_INNER = X_HEADS * X_HD
HYB_IN = 6160
HYB_PAD = 6272
LANE = 128
SUB = 8
ADAM_LR = 0.001
ADAM_B1 = 0.9
ADAM_B2 = 0.999
ADAM_EPS = 1e-08
ADAM_WD = 0.01
ADAM_STEP = 10
PACK_COLS = 1024
VMEM_LIMIT = 56 * 1024 * 1024


def _params(n_grid):
    return pltpu.CompilerParams(dimension_semantics=("arbitrary",) * n_grid, vmem_limit_bytes=VMEM_LIMIT)


def _dg(a, b, ca, cb, hi):
    dims = (((ca,), (cb,)), ((), ()))
    if hi:
        return lax.dot_general(a, b, dims, precision=HIGHEST, preferred_element_type=F32)
    return lax.dot_general(a.astype(BF16), b.astype(BF16), dims, preferred_element_type=F32)


@functools.partial(jax.custom_vjp, nondiff_argnums=(2, 3, 4))
def mm(a, b, ca, cb, hi):
    return _dg(a, b, ca, cb, hi)


def _mm_fwd(a, b, ca, cb, hi):
    return _dg(a, b, ca, cb, hi), (a, b)


def _mm_bwd(ca, cb, hi, res, g):
    a, b = res
    if ca == 1:
        da = _dg(g, b, 1, 1 - cb, hi)
    else:
        da = _dg(b, g, 1 - cb, 1, hi)
    if cb == 0:
        db = _dg(a, g, 1 - ca, 0, hi)
    else:
        db = _dg(g, a, 0, 1 - ca, hi)
    return da, db


mm.defvjp(_mm_fwd, _mm_bwd)


def mm_nn(a, b, hi=False):
    return mm(a, b, 1, 0, hi)


def mm_nt(a, b, hi=False):
    return mm(a, b, 1, 1, hi)


def mm_tn(a, b, hi=False):
    return mm(a, b, 0, 0, hi)


def _rows(shape):
    return lax.broadcasted_iota(jnp.int32, shape, 0)


def _lanes(shape):
    return lax.broadcasted_iota(jnp.int32, shape, 1)


@functools.partial(jax.custom_vjp, nondiff_argnums=(2,))
def shift_rows(x, tail, k):
    rx = pltpu.roll(x, k, 0)
    rt = pltpu.roll(tail, k, 0)
    first = jnp.where(_rows(tail.shape) < k, rt, rx[:SUB])
    return jnp.concatenate([first, rx[SUB:]], axis=0)


def _shift_rows_fwd(x, tail, k):
    return shift_rows(x, tail, k), None


def _shift_rows_bwd(k, _, g):
    n = g.shape[0]
    dx = jnp.where(_rows(g.shape) < n - k, pltpu.roll(g, n - k, 0), 0.0)
    g8 = g[:SUB]
    dtail = jnp.where(_rows(g8.shape) >= SUB - k, pltpu.roll(g8, SUB - k, 0), 0.0)
    return dx, dtail


shift_rows.defvjp(_shift_rows_fwd, _shift_rows_bwd)


@functools.partial(jax.custom_vjp, nondiff_argnums=(1, 2))
def shift_fill(x, k, fill):
    return jnp.where(_rows(x.shape) >= k, pltpu.roll(x, k, 0), fill)


def _shift_fill_fwd(x, k, fill):
    return shift_fill(x, k, fill), None


def _shift_fill_bwd(k, fill, _, g):
    n = g.shape[0]
    return (jnp.where(_rows(g.shape) < n - k, pltpu.roll(g, n - k, 0), 0.0),)


shift_fill.defvjp(_shift_fill_fwd, _shift_fill_bwd)


def _row_of(w, j):
    return jnp.sum(jnp.where(_rows(w.shape) == j, w, 0.0), axis=0, keepdims=True)


def _lane_col(x, j):
    return jnp.sum(jnp.where(_lanes(x.shape) == j, x, 0.0), axis=1, keepdims=True)


def _scalar_at(w, r, j):
    m = (_rows(w.shape) == r) & (_lanes(w.shape) == j)
    s = jnp.sum(jnp.where(m, w, 0.0), axis=1, keepdims=True)
    return jnp.sum(s, axis=0, keepdims=True)


def _col2row(col, n):
    eye = _rows((n, n)) == _lanes((n, n))
    return jnp.sum(jnp.where(eye, jnp.broadcast_to(col, (n, n)), 0.0), axis=0, keepdims=True)


def _rms(x, g):
    return x * lax.rsqrt(jnp.mean(x * x, axis=-1, keepdims=True) + EPS) * g


def _conv_core(x, tail, w, width):
    y = x * _row_of(w, width - 1)
    for k in range(1, width):
        y = y + shift_rows(x, tail, k) * _row_of(w, width - 1 - k)
    return y


def _bf16_round(a):
    return a.astype(BF16).astype(F32)


@functools.partial(jax.custom_vjp, nondiff_argnums=(3,))
def _conv(x, tail, w, width):
    return _conv_core(_bf16_round(x), _bf16_round(tail), _bf16_round(w), width)


def _conv_fwd(x, tail, w, width):
    xb, tb, wb = _bf16_round(x), _bf16_round(tail), _bf16_round(w)
    return _conv_core(xb, tb, wb, width), (xb, tb, wb)


def _conv_bwd(width, res, g):
    _, vjp = jax.vjp(functools.partial(_conv_core, width=width), *res)
    return vjp(_bf16_round(g))


_conv.defvjp(_conv_fwd, _conv_bwd)


def _neg_expm1(x):
    t = jnp.tanh(0.5 * x)
    return -2.0 * t / (1.0 - t)


@jax.custom_vjp
def inv_unit_lower_all(ls):
    n = ls[0].shape[0]
    eye = (_rows(ls[0].shape) == _lanes(ls[0].shape)).astype(F32)
    ts = [eye - l for l in ls]
    ps = list(ls)
    k = 2
    while k < n:
        ps = [_dg(p, p, 1, 0, True) for p in ps]
        ts = [_dg(t, eye + p, 1, 0, True) for t, p in zip(ts, ps)]
        k *= 2
    return ts


def _inv_all_fwd(ls):
    ts = inv_unit_lower_all(ls)
    return ts, ts


def _inv_all_bwd(ts, gs):
    xs = [_dg(g, t, 1, 1, True) for g, t in zip(gs, ts)]
    return ([-_dg(t, x, 0, 0, True) for t, x in zip(ts, xs)],)


inv_unit_lower_all.defvjp(_inv_all_fwd, _inv_all_bwd)


def _gdn_chunks(per_head, ba, sc, ng, head0):
    c = GDN_CHUNK
    n = len(per_head)
    hs = range(n)
    qr, kr, vr, tq, tk, tv, z, wq, wk, wv, state = [list(col) for col in zip(*per_head)]
    qc = [jax.nn.silu(_conv(qr[i], tq[i], wq[i], 4)) for i in hs]
    kc = [jax.nn.silu(_conv(kr[i], tk[i], wk[i], 4)) for i in hs]
    v = [jax.nn.silu(_conv(vr[i], tv[i], wv[i], 4)) for i in hs]
    q = [x * lax.rsqrt(jnp.sum(x * x, axis=-1, keepdims=True) + EPS) * (GDN_DK ** -0.5) for x in qc]
    k = [x * lax.rsqrt(jnp.sum(x * x, axis=-1, keepdims=True) + EPS) for x in kc]
    beta = [jax.nn.sigmoid(_lane_col(ba, head0 + i)) for i in hs]
    gl = [-jnp.exp(_scalar_at(sc, 0, head0 + i))
          * jax.nn.softplus(_lane_col(ba, head0 + i + GDN_HEADS) + _scalar_at(sc, 1, head0 + i)) for i in hs]
    causal = _rows((c, c)) >= _lanes((c, c))
    strict = _rows((c, c)) > _lanes((c, c))
    gc = [jnp.sum(jnp.where(causal, _col2row(g, c), 0.0), axis=1, keepdims=True) for g in gl]
    decay = [jnp.exp(jnp.where(causal, g - _col2row(g, c), -jnp.inf)) for g in gc]
    kb = [k[i] * beta[i] for i in hs]
    kk = [mm_nt(kb[i], k[i]) for i in hs]
    qk = [mm_nt(q[i], k[i]) for i in hs]
    tmat = inv_unit_lower_all([jnp.where(strict, kk[i] * decay[i], 0.0) for i in hs])
    eg = [jnp.exp(g) for g in gc]
    u = [mm_nn(tmat[i], v[i] * beta[i]) for i in hs]
    w = [mm_nn(tmat[i], kb[i] * eg[i]) for i in hs]
    attn = [jnp.where(causal, qk[i] * decay[i], 0.0) for i in hs]
    g_last = [jnp.sum(g, axis=0, keepdims=True) for g in gl]
    ws = [mm_nn(w[i], state[i]) for i in hs]
    qs = [mm_nn(q[i] * eg[i], state[i]) for i in hs]
    v_new = [u[i] - ws[i] for i in hs]
    av = [mm_nn(attn[i], v_new[i]) for i in hs]
    kv = [mm_tn(k[i] * jnp.exp(g_last[i] - gc[i]), v_new[i]) for i in hs]
    out = [_rms(qs[i] + av[i], ng) * jax.nn.silu(z[i]) for i in hs]
    new_state = [state[i] * jnp.exp(g_last[i]) + kv[i] for i in hs]
    return list(zip(out, new_state))


def _lru_block(lx, tail, lg, cw, cb, wr, br, wi, bi, lam, hprev):
    n = lx.shape[0]
    xc = _conv(lx, tail, cw, 4) + cb
    r = jax.nn.sigmoid(mm_nn(xc, wr) + br)
    i = jax.nn.sigmoid(mm_nn(xc, wi) + bi)
    log_a = -LRU_C * r * jax.nn.softplus(-lam)
    a = jnp.exp(log_a)
    b = jnp.sqrt(_neg_expm1(2.0 * log_a)) * (i * xc)
    k = 1
    while k < n:
        b = a * shift_fill(b, k, 0.0) + b
        a = a * shift_fill(a, k, 1.0)
        k *= 2
    h = b + a * hprev
    hlast = jnp.sum(jnp.where(_rows(h.shape) == n - 1, h, 0.0), axis=0, keepdims=True)
    return h * jax.nn.gelu(lg), hlast


def _signed_perm(x, rot, cb):
    hi = x.astype(BF16)
    r1 = x - hi.astype(F32)
    mid = r1.astype(BF16)
    lo = (r1 - mid.astype(F32)).astype(BF16)
    rb = rot.astype(BF16)
    dims = (((1,), (cb,)), ((), ()))
    dot = functools.partial(lax.dot_general, dimension_numbers=dims, preferred_element_type=F32)
    return (dot(hi, rb) + dot(mid, rb)) + dot(lo, rb)


@jax.custom_vjp
def _rotate_half(x, rot):
    return _signed_perm(x, rot, 0)


def _rotate_half_fwd(x, rot):
    return _signed_perm(x, rot, 0), rot


def _rotate_half_bwd(rot, g):
    return _signed_perm(g, rot, 1), jnp.zeros_like(rot)


_rotate_half.defvjp(_rotate_half_fwd, _rotate_half_bwd)


def _rope(x, cos, sin, rot):
    return x * cos + _rotate_half(x, rot) * sin


def _swa_heads(groups, cosc, sinc, cosp, sinp, qn, kn, sinks, head0, not_first, rot):
    n = groups[0][1].shape[0]
    first_off = jnp.where(not_first, 0, n)
    scale = SWA_HD ** -0.5
    qi = _rows((n, n))
    kj = _lanes((n, n))
    kc_r = [_rope(_rms(grp[1], kn), cosc, sinc, rot) for grp in groups]
    kp_r = [_rope(_rms(grp[2], kn), cosp, sinp, rot) for grp in groups]
    heads = [(j, i) for j, grp in enumerate(groups) for i in range(len(grp[0]))]
    per = len(groups[0][0])
    q_r = [_rope(_rms(groups[j][0][i], qn), cosc, sinc, rot) for j, i in heads]
    s_c = [mm_nt(q_r[h], kc_r[j]) for h, (j, i) in enumerate(heads)]
    s_p = [mm_nt(q_r[h], kp_r[j]) for h, (j, i) in enumerate(heads)]
    s_c = [jnp.where(kj <= qi, s * scale, -jnp.inf) for s in s_c]
    s_p = [jnp.where(kj > qi + first_off, s * scale, -jnp.inf) for s in s_p]
    sink = [_scalar_at(sinks, 0, head0 + j * per + i) for j, i in heads]
    m = [lax.stop_gradient(jnp.maximum(jnp.maximum(jnp.max(s_c[h], axis=1, keepdims=True),
                                                   jnp.max(s_p[h], axis=1, keepdims=True)), sink[h]))
         for h in range(len(heads))]
    pc = [jnp.exp(s_c[h] - m[h]) for h in range(len(heads))]
    pp = [jnp.exp(s_p[h] - m[h]) for h in range(len(heads))]
    denom = [jnp.sum(pc[h], axis=1, keepdims=True) + jnp.sum(pp[h], axis=1, keepdims=True) + jnp.exp(sink[h] - m[h])
             for h in range(len(heads))]
    oc = [mm_nn(pc[h] / denom[h], groups[j][3]) for h, (j, i) in enumerate(heads)]
    op = [mm_nn(pp[h] / denom[h], groups[j][4]) for h, (j, i) in enumerate(heads)]
    outs = [oc[h] + op[h] for h in range(len(heads))]
    return [tuple(outs[j * per:(j + 1) * per]) for j in range(len(groups))]


def _xattn_head(q, k, v, qn, kn):
    qh = _rms(q, qn)
    kh = _rms(k, kn)
    s = mm_nt(qh, kh) * (X_HD ** -0.5)
    m = lax.stop_gradient(jnp.max(s, axis=1, keepdims=True))
    p = jnp.exp(s - m)
    p = p / jnp.sum(p, axis=1, keepdims=True)
    return mm_nn(p, v)


def _ffn_block(gate, tail, up, cw, cb):
    gt = _conv(gate, tail, cw, 3) + cb
    return jax.nn.silu(gt) * up


def _tile(n, cands):
    for c in cands:
        if n % c == 0:
            return c
    return n


def matmul(a, b, mode, res=None, out_dtype=F32, name="mm"):
    if mode == "nn":
        (m, k), (_, n) = a.shape, b.shape
    elif mode == "nt":
        (m, k), (n, _) = a.shape, b.shape
    else:
        (k, m), (_, n) = a.shape, b.shape
    tm = _tile(m, (1024, 1408, 896, 512, 256, 128))
    tn = _tile(n, (1408, 1024, 896, 768, 640, 512, 384, 256, 128))
    tk = _tile(k, (1024, 896, 512, 256, 128))
    nk = k // tk
    ca, cb = {"nn": (1, 0), "nt": (1, 1), "tn": (0, 0)}[mode]
    a_spec = pl.BlockSpec((tk, tm), lambda i, j, kk: (kk, i)) if mode == "tn" else pl.BlockSpec((tm, tk), lambda i, j, kk: (i, kk))
    b_spec = pl.BlockSpec((tn, tk), lambda i, j, kk: (j, kk)) if mode == "nt" else pl.BlockSpec((tk, tn), lambda i, j, kk: (kk, j))
    o_spec = pl.BlockSpec((tm, tn), lambda i, j, kk: (i, j))
    has_res = res is not None

    def body(*refs):
        if has_res:
            a_ref, b_ref, r_ref, o_ref, acc = refs
        else:
            a_ref, b_ref, o_ref, acc = refs
        kk = pl.program_id(2)

        @pl.when(kk == 0)
        def _():
            acc[...] = jnp.zeros_like(acc)

        acc[...] += _dg(a_ref[...], b_ref[...], ca, cb, False)

        @pl.when(kk == nk - 1)
        def _():
            r = acc[...]
            if has_res:
                r = r + r_ref[...].astype(F32)
            o_ref[...] = r.astype(o_ref.dtype)

    ins = [a, b] + ([res] if has_res else [])
    specs = [a_spec, b_spec] + ([o_spec] if has_res else [])
    return pl.pallas_call(
        body, name=name, grid=(m // tm, n // tn, nk), in_specs=specs, out_specs=o_spec,
        out_shape=jax.ShapeDtypeStruct((m, n), out_dtype), scratch_shapes=[pltpu.VMEM((tm, tn), F32)],
        compiler_params=_params(3))(*ins)


def norm_fwd(x, g, name="norm_fwd"):
    t, d = x.shape
    tb = _tile(t, (512, 256, 128, 64))

    def body(x_ref, g_ref, o_ref):
        o_ref[...] = _rms(x_ref[...], g_ref[...]).astype(o_ref.dtype)

    return pl.pallas_call(
        body, name=name, grid=(t // tb,),
        in_specs=[pl.BlockSpec((tb, d), lambda i: (i, 0)), pl.BlockSpec((1, d), lambda i: (0, 0))],
        out_specs=pl.BlockSpec((tb, d), lambda i: (i, 0)), out_shape=jax.ShapeDtypeStruct((t, d), BF16),
        compiler_params=_params(1))(x, g)


def norm_bwd(x, g, dh, dres=None, name="norm_bwd"):
    t, d = x.shape
    tb = _tile(t, (256, 128, 64))
    with_dx = dres is not None

    def body(*refs):
        if with_dx:
            x_ref, g_ref, dh_ref, dr_ref, dx_ref, dg_ref = refs
        else:
            x_ref, g_ref, dh_ref, dg_ref = refs
        _, vjp = jax.vjp(_rms, x_ref[...], g_ref[...])
        dx, dg = vjp(dh_ref[...].astype(F32))

        @pl.when(pl.program_id(0) == 0)
        def _():
            dg_ref[...] = jnp.zeros_like(dg_ref)

        dg_ref[...] += dg
        if with_dx:
            dx_ref[...] = dx + dr_ref[...]

    blk = pl.BlockSpec((tb, d), lambda i: (i, 0))
    gsp = pl.BlockSpec((1, d), lambda i: (0, 0))
    if with_dx:
        return pl.pallas_call(
            body, name=name, grid=(t // tb,), in_specs=[blk, gsp, blk, blk], out_specs=(blk, gsp),
            out_shape=(jax.ShapeDtypeStruct((t, d), F32), jax.ShapeDtypeStruct((1, d), F32)),
            compiler_params=_params(1))(x, g, dh, dres)
    return pl.pallas_call(
        body, name=name, grid=(t // tb,), in_specs=[blk, gsp, blk], out_specs=gsp,
        out_shape=jax.ShapeDtypeStruct((1, d), F32), compiler_params=_params(1))(x, g, dh)


def loss_grad(y, target):
    t, d = y.shape
    tb = _tile(t, (512, 256, 128, 64))

    def body(y_ref, t_ref, dy_ref, l_ref):
        err = y_ref[...] - t_ref[...]
        dy_ref[...] = err / d

        @pl.when(pl.program_id(0) == 0)
        def _():
            l_ref[...] = jnp.zeros_like(l_ref)

        part = jnp.sum(jnp.mean(err * err, axis=-1, keepdims=True), axis=0, keepdims=True)
        l_ref[...] += 0.5 * part

    blk = pl.BlockSpec((tb, d), lambda i: (i, 0))
    lsp = pl.BlockSpec((SUB, LANE), lambda i: (0, 0))
    return pl.pallas_call(
        body, name="loss_grad", grid=(t // tb,), in_specs=[blk, blk], out_specs=(blk, lsp),
        out_shape=(jax.ShapeDtypeStruct((t, d), F32), jax.ShapeDtypeStruct((SUB, LANE), F32)),
        compiler_params=_params(1))(y, target)


def _tail_map(tb, col):
    return lambda g, t: (jnp.maximum(t * (tb // SUB) - 1, 0), col(g))


def _tail_map_rev(tb, nb, col):
    return lambda g, t: (jnp.maximum((nb - 1 - t) * (tb // SUB) - 1, 0), col(g))


GDN_HPB = 4

def gdn_fwd(p, conv_w, sc, ng):
    t = p.shape[0]
    tb = GDN_CHUNK
    nb = t // tb
    h = GDN_HEADS
    hpb = GDN_HPB
    ng_ = h // hpb
    wid = hpb * LANE

    def body(q_ref, k_ref, v_ref, tq_ref, tk_ref, tv_ref, z_ref, ba_ref, wq_ref, wk_ref, wv_ref, sc_ref, ng_ref,
             o_ref, s_ref, state):
        g, ti = pl.program_id(0), pl.program_id(1)

        @pl.when(ti == 0)
        def _():
            state[...] = jnp.zeros_like(state)

        live = (ti > 0).astype(F32)
        ba, sc_, gain = ba_ref[...], sc_ref[...], ng_ref[...]
        sls = [slice(hh * LANE, (hh + 1) * LANE) for hh in range(hpb)]
        ins = [(q_ref[:, sl], k_ref[:, sl], v_ref[:, sl], tq_ref[:, sl] * live, tk_ref[:, sl] * live, tv_ref[:, sl] * live,
                z_ref[:, sl], wq_ref[:, sl], wk_ref[:, sl], wv_ref[:, sl], state[hh]) for hh, sl in enumerate(sls)]
        outs = _gdn_chunks(ins, ba, sc_, gain, g * hpb)
        for hh, sl in enumerate(sls):
            s_ref[hh] = ins[hh][-1]
            o_ref[:, sl] = outs[hh][0].astype(o_ref.dtype)
            state[hh] = outs[hh][1]

    def blk(off):
        return pl.BlockSpec((tb, wid), lambda g, ti: (ti, off + g))

    def tail(off):
        return pl.BlockSpec((SUB, wid), _tail_map(tb, lambda g: off + g))

    def wsp(off):
        return pl.BlockSpec((4, wid), lambda g, ti: (0, off + g))

    in_specs = [blk(0), blk(ng_), blk(2 * ng_), tail(0), tail(ng_), tail(2 * ng_), blk(3 * ng_),
                pl.BlockSpec((tb, LANE), lambda g, ti: (ti, 6 * h)), wsp(0), wsp(ng_), wsp(2 * ng_),
                pl.BlockSpec((SUB, LANE), lambda g, ti: (0, 0)), pl.BlockSpec((1, LANE), lambda g, ti: (0, 0))]
    out_specs = (pl.BlockSpec((tb, wid), lambda g, ti: (ti, g)),
                 pl.BlockSpec((hpb, None, LANE, LANE), lambda g, ti: (g, ti, 0, 0)))
    return pl.pallas_call(
        body, name="gdn_fwd", grid=(ng_, nb), in_specs=in_specs, out_specs=out_specs,
        out_shape=(jax.ShapeDtypeStruct((t, GDN_V), BF16), jax.ShapeDtypeStruct((h, nb, LANE, LANE), F32)),
        scratch_shapes=[pltpu.VMEM((hpb, LANE, LANE), F32)], compiler_params=_params(2))(
            p, p, p, p, p, p, p, p, conv_w, conv_w, conv_w, sc, ng)


def gdn_bwd(p, conv_w, sc, ng, states, dmix):
    t = p.shape[0]
    tb = GDN_CHUNK
    nb = t // tb
    h = GDN_HEADS
    hpb = GDN_HPB
    ng_ = h // hpb
    wid = hpb * LANE

    def body(q_ref, k_ref, v_ref, tq_ref, tk_ref, tv_ref, z_ref, ba_ref, wq_ref, wk_ref, wv_ref, sc_ref, ng_ref,
             s_ref, do_ref, dq_ref, dk_ref, dv_ref, dz_ref, dba_ref, dwq_ref, dwk_ref, dwv_ref, dsc_ref, dng_ref,
             dstate, cq, ck, cv):
        g, step = pl.program_id(0), pl.program_id(1)
        head = g
        ti = nb - 1 - step

        @pl.when(step == 0)
        def _():
            dstate[...] = jnp.zeros_like(dstate)
            cq[...] = jnp.zeros_like(cq)
            ck[...] = jnp.zeros_like(ck)
            cv[...] = jnp.zeros_like(cv)
            dwq_ref[...] = jnp.zeros_like(dwq_ref)
            dwk_ref[...] = jnp.zeros_like(dwk_ref)
            dwv_ref[...] = jnp.zeros_like(dwv_ref)

        @pl.when((step == 0) & (head == 0))
        def _():
            dsc_ref[...] = jnp.zeros_like(dsc_ref)
            dng_ref[...] = jnp.zeros_like(dng_ref)

        live = (ti > 0).astype(F32)
        ba, sc_, gain = ba_ref[...], sc_ref[...], ng_ref[...]
        pad = jnp.zeros((tb - SUB, LANE), F32)
        sls = [slice(hh * LANE, (hh + 1) * LANE) for hh in range(hpb)]
        ins = [(q_ref[:, sl], k_ref[:, sl], v_ref[:, sl], tq_ref[:, sl] * live, tk_ref[:, sl] * live, tv_ref[:, sl] * live,
                z_ref[:, sl], wq_ref[:, sl], wk_ref[:, sl], wv_ref[:, sl], s_ref[hh]) for hh, sl in enumerate(sls)]
        cots = [(do_ref[:, sl], dstate[hh]) for hh, sl in enumerate(sls)]
        carry = [(cq[:, sl], ck[:, sl], cv[:, sl]) for sl in sls]
        _, vjp = jax.vjp(functools.partial(_gdn_chunks, head0=g * hpb), ins, ba, sc_, gain)
        grads, dba, dsc_sum, dng_sum = vjp(cots)
        dba_ref[...] = dba
        for hh, sl in enumerate(sls):
            (dq, dk, dv, dtq, dtk, dtv, dz, dwq, dwk, dwv, ds) = grads[hh]
            dq_ref[:, sl] = (dq + jnp.concatenate([pad, carry[hh][0]], axis=0)).astype(dq_ref.dtype)
            dk_ref[:, sl] = (dk + jnp.concatenate([pad, carry[hh][1]], axis=0)).astype(dk_ref.dtype)
            dv_ref[:, sl] = (dv + jnp.concatenate([pad, carry[hh][2]], axis=0)).astype(dv_ref.dtype)
            cq[:, sl] = dtq * live
            ck[:, sl] = dtk * live
            cv[:, sl] = dtv * live
            dz_ref[:, sl] = dz.astype(dz_ref.dtype)
            dwq_ref[:, sl] += dwq
            dwk_ref[:, sl] += dwk
            dwv_ref[:, sl] += dwv
            dstate[hh] = ds
        dsc_ref[...] += dsc_sum
        dng_ref[...] += dng_sum

    def blk(off):
        return pl.BlockSpec((tb, wid), lambda g, s: (nb - 1 - s, off + g))

    def tail(off):
        return pl.BlockSpec((SUB, wid), _tail_map_rev(tb, nb, lambda g: off + g))

    def wsp(off):
        return pl.BlockSpec((4, wid), lambda g, s: (0, off + g))

    shared8 = pl.BlockSpec((SUB, LANE), lambda g, s: (0, 0))
    shared1 = pl.BlockSpec((1, LANE), lambda g, s: (0, 0))
    in_specs = [blk(0), blk(ng_), blk(2 * ng_), tail(0), tail(ng_), tail(2 * ng_), blk(3 * ng_),
                pl.BlockSpec((tb, LANE), lambda g, s: (nb - 1 - s, 6 * h)), wsp(0), wsp(ng_), wsp(2 * ng_), shared8, shared1,
                pl.BlockSpec((hpb, None, LANE, LANE), lambda g, s: (g, nb - 1 - s, 0, 0)), blk(0)]
    oblk = pl.BlockSpec((tb, wid), lambda g, s: (nb - 1 - s, g))
    wacc = pl.BlockSpec((4, wid), lambda g, s: (0, g))
    out_specs = (oblk, oblk, oblk, oblk, pl.BlockSpec((None, tb, LANE), lambda g, s: (g, nb - 1 - s, 0)),
                 wacc, wacc, wacc, shared8, shared1)
    act = jax.ShapeDtypeStruct((t, GDN_V), BF16)
    wsh = jax.ShapeDtypeStruct((4, GDN_V), F32)
    out_shape = (act, act, act, act, jax.ShapeDtypeStruct((ng_, t, LANE), F32), wsh, wsh, wsh,
                 jax.ShapeDtypeStruct((SUB, LANE), F32), jax.ShapeDtypeStruct((1, LANE), F32))
    scratch = [pltpu.VMEM((hpb, LANE, LANE), F32)] + [pltpu.VMEM((SUB, wid), F32)] * 3
    return pl.pallas_call(
        body, name="gdn_bwd", grid=(ng_, nb), in_specs=in_specs, out_specs=out_specs, out_shape=out_shape,
        scratch_shapes=scratch, compiler_params=_params(2))(
            p, p, p, p, p, p, p, p, conv_w, conv_w, conv_w, sc, ng, states, dmix)


LRU_TB = 256
LRU_X_OFF = 32
LRU_G_OFF = 40


def lru_fwd(p, cw, cb, wr, br, wi, bi, lam):
    t = p.shape[0]
    tb = min(LRU_TB, t)
    nb = t // tb
    g8 = LRU_BLOCKS

    def body(x_ref, tx_ref, lg_ref, cw_ref, cb_ref, wr_ref, br_ref, wi_ref, bi_ref, lam_ref, o_ref, hs_ref, hstate):
        ti = pl.program_id(1)

        @pl.when(ti == 0)
        def _():
            hstate[...] = jnp.zeros_like(hstate)

        live = (ti > 0).astype(F32)
        hs_ref[...] = hstate[...]
        o, hlast = _lru_block(x_ref[...], tx_ref[...] * live, lg_ref[...], cw_ref[...], cb_ref[...], wr_ref[...],
                              br_ref[...], wi_ref[...], bi_ref[...], lam_ref[...], hstate[0:1, :])
        o_ref[...] = o.astype(o_ref.dtype)
        hstate[...] = jnp.broadcast_to(hlast, hstate.shape)

    vec = pl.BlockSpec((1, LANE), lambda g, ti: (0, g))
    mat = pl.BlockSpec((None, LANE, LANE), lambda g, ti: (g, 0, 0))
    in_specs = [pl.BlockSpec((tb, LANE), lambda g, ti: (ti, LRU_X_OFF + g)),
                pl.BlockSpec((SUB, LANE), _tail_map(tb, lambda g: LRU_X_OFF + g)),
                pl.BlockSpec((tb, LANE), lambda g, ti: (ti, LRU_G_OFF + g)),
                pl.BlockSpec((4, LANE), lambda g, ti: (0, g)), vec, mat, vec, mat, vec, vec]
    out_specs = (pl.BlockSpec((tb, LANE), lambda g, ti: (ti, g)),
                 pl.BlockSpec((None, None, SUB, LANE), lambda g, ti: (g, ti, 0, 0)))
    return pl.pallas_call(
        body, name="lru_fwd", grid=(g8, nb), in_specs=in_specs, out_specs=out_specs,
        out_shape=(jax.ShapeDtypeStruct((t, LRU_WIDTH), BF16), jax.ShapeDtypeStruct((g8, nb, SUB, LANE), F32)),
        scratch_shapes=[pltpu.VMEM((SUB, LANE), F32)], compiler_params=_params(2))(
            p, p, p, cw, cb, wr, br, wi, bi, lam)


def lru_bwd(p, cw, cb, wr, br, wi, bi, lam, hs, dmix):
    t = p.shape[0]
    tb = min(LRU_TB, t)
    nb = t // tb
    g8 = LRU_BLOCKS

    def body(x_ref, tx_ref, lg_ref, cw_ref, cb_ref, wr_ref, br_ref, wi_ref, bi_ref, lam_ref, hs_ref, do_ref,
             dx_ref, dlg_ref, dcw_ref, dcb_ref, dwr_ref, dbr_ref, dwi_ref, dbi_ref, dlam_ref, dh, cx):
        step = pl.program_id(1)
        ti = nb - 1 - step

        @pl.when(step == 0)
        def _():
            dh[...] = jnp.zeros_like(dh)
            cx[...] = jnp.zeros_like(cx)
            for r in (dcw_ref, dcb_ref, dwr_ref, dbr_ref, dwi_ref, dbi_ref, dlam_ref):
                r[...] = jnp.zeros_like(r)

        live = (ti > 0).astype(F32)
        _, vjp = jax.vjp(_lru_block, x_ref[...], tx_ref[...] * live, lg_ref[...], cw_ref[...], cb_ref[...], wr_ref[...],
                         br_ref[...], wi_ref[...], bi_ref[...], lam_ref[...], hs_ref[0:1, :])
        dx, dtx, dlg, dcw, dcb, dwr, dbr, dwi, dbi, dlam, dhp = vjp((do_ref[...], dh[0:1, :]))
        pad = jnp.zeros((tb - SUB, LANE), F32)
        dx_ref[...] = (dx + jnp.concatenate([pad, cx[...]], axis=0)).astype(dx_ref.dtype)
        cx[...] = dtx * live
        dlg_ref[...] = dlg.astype(dlg_ref.dtype)
        dcw_ref[...] += dcw
        dcb_ref[...] += dcb
        dwr_ref[...] += dwr
        dbr_ref[...] += dbr
        dwi_ref[...] += dwi
        dbi_ref[...] += dbi
        dlam_ref[...] += dlam
        dh[...] = jnp.broadcast_to(dhp, dh.shape)

    vec = pl.BlockSpec((1, LANE), lambda g, s: (0, g))
    mat = pl.BlockSpec((None, LANE, LANE), lambda g, s: (g, 0, 0))
    cwsp = pl.BlockSpec((4, LANE), lambda g, s: (0, g))
    in_specs = [pl.BlockSpec((tb, LANE), lambda g, s: (nb - 1 - s, LRU_X_OFF + g)),
                pl.BlockSpec((SUB, LANE), _tail_map_rev(tb, nb, lambda g: LRU_X_OFF + g)),
                pl.BlockSpec((tb, LANE), lambda g, s: (nb - 1 - s, LRU_G_OFF + g)),
                cwsp, vec, mat, vec, mat, vec, vec,
                pl.BlockSpec((None, None, SUB, LANE), lambda g, s: (g, nb - 1 - s, 0, 0)),
                pl.BlockSpec((tb, LANE), lambda g, s: (nb - 1 - s, LRU_BLOCKS + g))]
    oblk = pl.BlockSpec((tb, LANE), lambda g, s: (nb - 1 - s, g))
    out_specs = (oblk, oblk, cwsp, vec, mat, vec, mat, vec, vec)
    act = jax.ShapeDtypeStruct((t, LRU_WIDTH), BF16)
    vsh = jax.ShapeDtypeStruct((1, LRU_WIDTH), F32)
    msh = jax.ShapeDtypeStruct((g8, LANE, LANE), F32)
    out_shape = (act, act, jax.ShapeDtypeStruct((4, LRU_WIDTH), F32), vsh, msh, vsh, msh, vsh, vsh)
    return pl.pallas_call(
        body, name="lru_bwd", grid=(g8, nb), in_specs=in_specs, out_specs=out_specs, out_shape=out_shape,
        scratch_shapes=[pltpu.VMEM((SUB, LANE), F32), pltpu.VMEM((SUB, LANE), F32)], compiler_params=_params(2))(
            p, p, p, cw, cb, wr, br, wi, bi, lam, hs, dmix)


SWA_GROUPS = 4
SWA_QW = SWA_Q // SWA_GROUPS
SWA_K_OFF = SWA_Q // LANE
SWA_V_OFF = (SWA_Q + SWA_KV) // LANE


def rope_tables(pos):
    t = pos.shape[0]
    tb = _tile(t, (512, 256, 128))
    inv = 1.0 / (ROPE_THETA ** (jnp.arange(0, SWA_HD, 2, dtype=F32) / SWA_HD))
    inv = jnp.concatenate([inv, inv]).reshape(1, SWA_HD)

    def body(p_ref, i_ref, c_ref, s_ref):
        ang = p_ref[...] * i_ref[...]
        c_ref[...] = jnp.cos(ang)
        s_ref[...] = jnp.sin(ang)

    blk = pl.BlockSpec((tb, SWA_HD), lambda i: (i, 0))
    sh = jax.ShapeDtypeStruct((t, SWA_HD), F32)
    return pl.pallas_call(
        body, name="rope_tables", grid=(t // tb,),
        in_specs=[pl.BlockSpec((tb, 1), lambda i: (i, 0)), pl.BlockSpec((1, SWA_HD), lambda i: (0, 0))],
        out_specs=(blk, blk), out_shape=(sh, sh), compiler_params=_params(1))(pos, inv)


def _rot_matrix():
    r, c = _rows((SWA_HD, SWA_HD)), _lanes((SWA_HD, SWA_HD))
    half = SWA_HD // 2
    return jnp.where(r == c + half, -1.0, 0.0) + jnp.where(r + half == c, 1.0, 0.0)


def _swa_specs(tb, cur, prev):
    def q_sp():
        return pl.BlockSpec((tb, SWA_QW), lambda g, s: (cur(s), g))

    def kv(off, which):
        return pl.BlockSpec((tb, LANE), lambda g, s: (which(s), off + g))

    def tab(which):
        return pl.BlockSpec((tb, SWA_HD), lambda g, s: (which(s), 0))

    nrm = pl.BlockSpec((1, SWA_HD), lambda g, s: (0, 0))
    snk = pl.BlockSpec((1, LANE), lambda g, s: (0, 0))
    return [q_sp(), kv(SWA_K_OFF, cur), kv(SWA_K_OFF, prev), kv(SWA_V_OFF, cur), kv(SWA_V_OFF, prev),
            tab(cur), tab(cur), tab(prev), tab(prev), nrm, nrm, snk]


def swa_fwd(p, cos, sin, qn, kn, sinks):
    t = p.shape[0]
    tb = SWA_BLOCK
    nb = t // tb
    per = SWA_HEADS // SWA_KV_HEADS

    def body(q_ref, kc_ref, kp_ref, vc_ref, vp_ref, cc_ref, sc_ref, cp_ref, sp_ref, qn_ref, kn_ref, sk_ref, o_ref):
        g, ti = pl.program_id(0), pl.program_id(1)
        rot = _rot_matrix()
        tabs = (cc_ref[...], sc_ref[...], cp_ref[...], sp_ref[...], qn_ref[...], kn_ref[...], sk_ref[...])
        ksls = [slice(j * SWA_HD, (j + 1) * SWA_HD) for j in range(2)]
        ins = [([q_ref[:, (j * per + i) * SWA_HD:(j * per + i + 1) * SWA_HD] for i in range(per)],
                kc_ref[:, ksl], kp_ref[:, ksl], vc_ref[:, ksl], vp_ref[:, ksl]) for j, ksl in enumerate(ksls)]
        outs = _swa_heads(ins, *tabs, g * (2 * per), ti > 0, rot)
        for j in range(2):
            for i in range(per):
                o_ref[:, (j * per + i) * SWA_HD:(j * per + i + 1) * SWA_HD] = outs[j][i].astype(o_ref.dtype)

    in_specs = _swa_specs(tb, lambda s: s, lambda s: jnp.maximum(s - 1, 0))
    return pl.pallas_call(
        body, name="swa_fwd", grid=(SWA_GROUPS, nb), in_specs=in_specs,
        out_specs=pl.BlockSpec((tb, SWA_QW), lambda g, s: (s, g)), out_shape=jax.ShapeDtypeStruct((t, SWA_Q), BF16),
        compiler_params=_params(2))(p, p, p, p, p, cos, sin, cos, sin, qn, kn, sinks)


def swa_bwd(p, cos, sin, qn, kn, sinks, do):
    t = p.shape[0]
    tb = SWA_BLOCK
    nb = t // tb
    per = SWA_HEADS // SWA_KV_HEADS

    def body(q_ref, kc_ref, kp_ref, vc_ref, vp_ref, cc_ref, sc_ref, cp_ref, sp_ref, qn_ref, kn_ref, sk_ref, do_ref,
             dq_ref, dk_ref, dv_ref, dqn_ref, dkn_ref, dsk_ref, ck, cv):
        g, step = pl.program_id(0), pl.program_id(1)
        ti = nb - 1 - step
        rot = _rot_matrix()

        @pl.when(step == 0)
        def _():
            ck[...] = jnp.zeros_like(ck)
            cv[...] = jnp.zeros_like(cv)

        @pl.when((step == 0) & (g == 0))
        def _():
            dqn_ref[...] = jnp.zeros_like(dqn_ref)
            dkn_ref[...] = jnp.zeros_like(dkn_ref)
            dsk_ref[...] = jnp.zeros_like(dsk_ref)

        cc, sc, cp, sp = cc_ref[...], sc_ref[...], cp_ref[...], sp_ref[...]
        qn_v, kn_v, sk_v = qn_ref[...], kn_ref[...], sk_ref[...]
        ksls = [slice(j * SWA_HD, (j + 1) * SWA_HD) for j in range(2)]
        ins = [([q_ref[:, (j * per + i) * SWA_HD:(j * per + i + 1) * SWA_HD] for i in range(per)],
                kc_ref[:, ksl], kp_ref[:, ksl], vc_ref[:, ksl], vp_ref[:, ksl]) for j, ksl in enumerate(ksls)]
        dos = [tuple(do_ref[:, (j * per + i) * SWA_HD:(j * per + i + 1) * SWA_HD].astype(F32) for i in range(per))
               for j in range(2)]
        carry = [(ck[:, ksl], cv[:, ksl]) for ksl in ksls]
        def fn(groups, qn_, kn_, sk):
            return _swa_heads(groups, cc, sc, cp, sp, qn_, kn_, sk, g * (2 * per), ti > 0, rot)

        _, vjp = jax.vjp(fn, ins, qn_v, kn_v, sk_v)
        grads, dqn, dkn, dsk = vjp(dos)
        for j, ksl in enumerate(ksls):
            dqs, dkc, dkp, dvc, dvp = grads[j]
            for i in range(per):
                dq_ref[:, (j * per + i) * SWA_HD:(j * per + i + 1) * SWA_HD] = dqs[i].astype(dq_ref.dtype)
            dk_ref[:, ksl] = (dkc + carry[j][0]).astype(dk_ref.dtype)
            dv_ref[:, ksl] = (dvc + carry[j][1]).astype(dv_ref.dtype)
            ck[:, ksl] = dkp
            cv[:, ksl] = dvp
        dqn_ref[...] += dqn
        dkn_ref[...] += dkn
        dsk_ref[...] += dsk

    in_specs = _swa_specs(tb, lambda s: nb - 1 - s, lambda s: jnp.maximum(nb - 2 - s, 0))
    in_specs.append(pl.BlockSpec((tb, SWA_QW), lambda g, s: (nb - 1 - s, g)))
    kvo = pl.BlockSpec((tb, LANE), lambda g, s: (nb - 1 - s, g))
    nrm = pl.BlockSpec((1, SWA_HD), lambda g, s: (0, 0))
    out_specs = (pl.BlockSpec((tb, SWA_QW), lambda g, s: (nb - 1 - s, g)), kvo, kvo, nrm, nrm,
                 pl.BlockSpec((1, LANE), lambda g, s: (0, 0)))
    out_shape = (jax.ShapeDtypeStruct((t, SWA_Q), BF16), jax.ShapeDtypeStruct((t, SWA_KV), BF16),
                 jax.ShapeDtypeStruct((t, SWA_KV), BF16), jax.ShapeDtypeStruct((1, SWA_HD), F32),
                 jax.ShapeDtypeStruct((1, SWA_HD), F32), jax.ShapeDtypeStruct((1, LANE), F32))
    return pl.pallas_call(
        body, name="swa_bwd", grid=(SWA_GROUPS, nb), in_specs=in_specs, out_specs=out_specs, out_shape=out_shape,
        scratch_shapes=[pltpu.VMEM((tb, LANE), F32), pltpu.VMEM((tb, LANE), F32)], compiler_params=_params(2))(
            p, p, p, p, p, cos, sin, cos, sin, qn, kn, sinks, do)


def xattn_fwd(q, kv, qn, kn):
    t = q.shape[0]
    mlen = kv.shape[0]
    tb = _tile(t, (512, 256, 128, 64))

    def body(q_ref, k_ref, v_ref, qn_ref, kn_ref, o_ref):
        o_ref[...] = _xattn_head(q_ref[...], k_ref[...], v_ref[...], qn_ref[...], kn_ref[...]).astype(o_ref.dtype)

    nrm = pl.BlockSpec((1, X_HD), lambda g, s: (0, 0))
    in_specs = [pl.BlockSpec((tb, X_HD), lambda g, s: (s, g)), pl.BlockSpec((mlen, X_HD), lambda g, s: (0, g)),
                pl.BlockSpec((mlen, X_HD), lambda g, s: (0, X_HEADS + g)), nrm, nrm]
    return pl.pallas_call(
        body, name="xattn_fwd", grid=(X_HEADS, t // tb), in_specs=in_specs,
        out_specs=pl.BlockSpec((tb, X_HD), lambda g, s: (s, g)), out_shape=jax.ShapeDtypeStruct((t, X_INNER), BF16),
        compiler_params=_params(2))(q, kv, kv, qn, kn)


def xattn_bwd(q, kv, qn, kn, do):
    t = q.shape[0]
    mlen = kv.shape[0]
    tb = _tile(t, (512, 256, 128, 64))

    def body(q_ref, k_ref, v_ref, qn_ref, kn_ref, do_ref, dq_ref, dk_ref, dv_ref, dqn_ref, dkn_ref):
        g, s = pl.program_id(0), pl.program_id(1)

        @pl.when(s == 0)
        def _():
            dk_ref[...] = jnp.zeros_like(dk_ref)
            dv_ref[...] = jnp.zeros_like(dv_ref)

        @pl.when((s == 0) & (g == 0))
        def _():
            dqn_ref[...] = jnp.zeros_like(dqn_ref)
            dkn_ref[...] = jnp.zeros_like(dkn_ref)

        _, vjp = jax.vjp(_xattn_head, q_ref[...], k_ref[...], v_ref[...], qn_ref[...], kn_ref[...])
        dq, dk, dv, dqn, dkn = vjp(do_ref[...].astype(F32))
        dq_ref[...] = dq.astype(dq_ref.dtype)
        dk_ref[...] += dk
        dv_ref[...] += dv
        dqn_ref[...] += dqn
        dkn_ref[...] += dkn

    nrm = pl.BlockSpec((1, X_HD), lambda g, s: (0, 0))
    qblk = pl.BlockSpec((tb, X_HD), lambda g, s: (s, g))
    kblk = pl.BlockSpec((mlen, X_HD), lambda g, s: (0, g))
    in_specs = [qblk, kblk, pl.BlockSpec((mlen, X_HD), lambda g, s: (0, X_HEADS + g)), nrm, nrm, qblk]
    out_specs = (qblk, kblk, kblk, nrm, nrm)
    msh = jax.ShapeDtypeStruct((mlen, X_INNER), F32)
    nsh = jax.ShapeDtypeStruct((1, X_HD), F32)
    return pl.pallas_call(
        body, name="xattn_bwd", grid=(X_HEADS, t // tb), in_specs=in_specs, out_specs=out_specs,
        out_shape=(jax.ShapeDtypeStruct((t, X_INNER), BF16), msh, msh, nsh, nsh), compiler_params=_params(2))(
            q, kv, kv, qn, kn, do)


FFN_CW = 512
FFN_TB = 512


def ffn_fwd(gu, cw, cb):
    t, f2 = gu.shape
    f = f2 // 2
    cwid = _tile(f, (FFN_CW, 256, 128))
    ng = f // cwid
    tb = min(FFN_TB, t)

    def body(g_ref, tg_ref, u_ref, cw_ref, cb_ref, o_ref):
        live = (pl.program_id(1) > 0).astype(F32)
        o_ref[...] = _ffn_block(g_ref[...], tg_ref[...] * live, u_ref[...], cw_ref[...], cb_ref[...]).astype(o_ref.dtype)

    in_specs = [pl.BlockSpec((tb, cwid), lambda g, s: (s, g)), pl.BlockSpec((SUB, cwid), _tail_map(tb, lambda g: g)),
                pl.BlockSpec((tb, cwid), lambda g, s: (s, ng + g)), pl.BlockSpec((3, cwid), lambda g, s: (0, g)),
                pl.BlockSpec((1, cwid), lambda g, s: (0, g))]
    return pl.pallas_call(
        body, name="ffn_fwd", grid=(ng, t // tb), in_specs=in_specs, out_specs=pl.BlockSpec((tb, cwid), lambda g, s: (s, g)),
        out_shape=jax.ShapeDtypeStruct((t, f), BF16), compiler_params=_params(2))(gu, gu, gu, cw, cb)


def ffn_bwd(gu, cw, cb, da):
    t, f2 = gu.shape
    f = f2 // 2
    cwid = _tile(f, (FFN_CW, 256, 128))
    ng = f // cwid
    tb = min(FFN_TB, t)
    nb = t // tb

    def body(g_ref, tg_ref, u_ref, cw_ref, cb_ref, da_ref, dg_ref, du_ref, dcw_ref, dcb_ref, cg):
        step = pl.program_id(1)
        ti = nb - 1 - step

        @pl.when(step == 0)
        def _():
            cg[...] = jnp.zeros_like(cg)
            dcw_ref[...] = jnp.zeros_like(dcw_ref)
            dcb_ref[...] = jnp.zeros_like(dcb_ref)

        live = (ti > 0).astype(F32)
        _, vjp = jax.vjp(_ffn_block, g_ref[...], tg_ref[...] * live, u_ref[...], cw_ref[...], cb_ref[...])
        dg, dtg, du, dcw, dcb = vjp(da_ref[...].astype(F32))
        pad = jnp.zeros((tb - SUB, cwid), F32)
        dg_ref[...] = (dg + jnp.concatenate([pad, cg[...]], axis=0)).astype(dg_ref.dtype)
        cg[...] = dtg * live
        du_ref[...] = du.astype(du_ref.dtype)
        dcw_ref[...] += dcw
        dcb_ref[...] += dcb

    blk = pl.BlockSpec((tb, cwid), lambda g, s: (nb - 1 - s, g))
    wsp = pl.BlockSpec((3, cwid), lambda g, s: (0, g))
    bsp = pl.BlockSpec((1, cwid), lambda g, s: (0, g))
    in_specs = [blk, pl.BlockSpec((SUB, cwid), _tail_map_rev(tb, nb, lambda g: g)),
                pl.BlockSpec((tb, cwid), lambda g, s: (nb - 1 - s, ng + g)), wsp, bsp, blk]
    act = jax.ShapeDtypeStruct((t, f), BF16)
    return pl.pallas_call(
        body, name="ffn_bwd", grid=(ng, nb), in_specs=in_specs, out_specs=(blk, blk, wsp, bsp),
        out_shape=(act, act, jax.ShapeDtypeStruct((3, f), F32), jax.ShapeDtypeStruct((1, f), F32)),
        scratch_shapes=[pltpu.VMEM((SUB, cwid), F32)], compiler_params=_params(2))(gu, gu, gu, cw, cb, da)


def adamw(parts, w, m, v, name="adamw"):
    _, r, c = parts.shape
    tr = r if r % 16 else _tile(r, (256, 128, 64, 32, 16))
    tc = c if c % LANE else _tile(c, (256, 128) if r % 16 else (512, 256, 128))

    def body(p_ref, w_ref, m_ref, v_ref, g_ref, d_ref, nm_ref, nv_ref):
        g = p_ref[0].astype(F32)
        for s in range(1, N_DEV):
            g = g + p_ref[s].astype(F32)
        m_new = ADAM_B1 * m_ref[...] + (1.0 - ADAM_B1) * g
        v_new = ADAM_B2 * v_ref[...] + (1.0 - ADAM_B2) * jnp.square(g)
        m_hat = m_new / (1.0 - ADAM_B1 ** ADAM_STEP)
        v_hat = v_new / (1.0 - ADAM_B2 ** ADAM_STEP)
        g_ref[...] = g
        d_ref[...] = -ADAM_LR * (m_hat / (jnp.sqrt(v_hat) + ADAM_EPS) + ADAM_WD * w_ref[...])
        nm_ref[...] = m_new
        nv_ref[...] = v_new

    blk = pl.BlockSpec((tr, tc), lambda i, j: (i, j))
    sh = jax.ShapeDtypeStruct((r, c), F32)
    return pl.pallas_call(
        body, name=name, grid=(r // tr, c // tc),
        in_specs=[pl.BlockSpec((N_DEV, tr, tc), lambda i, j: (0, i, j)), blk, blk, blk],
        out_specs=(blk, blk, blk, blk), out_shape=(sh, sh, sh, sh), compiler_params=_params(2))(parts, w, m, v)


def _me():
    return lax.axis_index("x"), lax.axis_index("y"), lax.axis_index("c")


def _flip(coords, k):
    x, y, c = coords
    return (1 - x if k & 4 else x, 1 - y if k & 2 else y, 1 - c if k & 1 else c)


def _slot(coords):
    x, y, c = coords
    return 4 * x + 2 * y + c


def _comm_call(body, name, ins, out_shapes):
    n = len(ins)
    hbm = pl.BlockSpec(memory_space=pl.ANY)
    return pl.pallas_call(
        body, name=name, out_shape=tuple(out_shapes), in_specs=[hbm] * n, out_specs=tuple([hbm] * n),
        scratch_shapes=[pltpu.SemaphoreType.DMA((7 * n,)), pltpu.SemaphoreType.DMA((7 * n,)), pltpu.SemaphoreType.DMA((n,))],
    )(*ins)


def all_gather(shards, name):
    n = len(shards)

    def body(*refs):
        x_refs, out_refs = refs[:n], refs[n:2 * n]
        send_sems, recv_sems, local_sems = refs[2 * n:]
        me = _me()
        sibling = _flip(me, 1)
        chips = [2, 4, 6]

        def copy(i, k, block, to, own=False):
            dst = out_refs[i].at[_slot(block)]
            return pltpu.make_async_remote_copy(
                src_ref=x_refs[i] if own else dst, dst_ref=dst, send_sem=send_sems.at[7 * i + k],
                recv_sem=recv_sems.at[7 * i + k], device_id=to, device_id_type=pl.DeviceIdType.MESH)

        mine = [pltpu.make_async_copy(x_refs[i], out_refs[i].at[_slot(me)], local_sems.at[i]) for i in range(n)]
        for cp in mine:
            cp.start()
        first = []
        for j, k in enumerate(chips):
            first += [copy(i, 1 + j, me, _flip(me, k), own=True) for i in range(n)]
        first += [copy(i, 0, me, sibling, own=True) for i in range(n)]
        for cp in first:
            cp.start()
        passed = []
        for j, k in enumerate(chips):
            for i in range(n):
                copy(i, 1 + j, _flip(me, k), me).wait_recv()
                cp = copy(i, 4 + j, _flip(me, k), sibling)
                cp.start()
                passed.append(cp)
        for i in range(n):
            copy(i, 0, sibling, me).wait_recv()
        for j, k in enumerate(chips):
            for i in range(n):
                copy(i, 4 + j, _flip(sibling, k), me).wait_recv()
        for cp in first + passed:
            cp.wait_send()
        for cp in mine:
            cp.wait()

    return _comm_call(body, name, shards, [jax.ShapeDtypeStruct((N_DEV,) + s.shape, s.dtype) for s in shards])


def all_to_all(blocks, name):
    n = len(blocks)

    def body(*refs):
        x_refs, out_refs = refs[:n], refs[n:2 * n]
        send_sems, recv_sems, local_sems = refs[2 * n:]
        me = _me()
        my_slot = _slot(me)
        mine = [pltpu.make_async_copy(x_refs[i].at[my_slot], out_refs[i].at[my_slot], local_sems.at[i]) for i in range(n)]
        for cp in mine:
            cp.start()

        def copy(i, k, peer):
            return pltpu.make_async_remote_copy(
                src_ref=x_refs[i].at[_slot(peer)], dst_ref=out_refs[i].at[my_slot], send_sem=send_sems.at[7 * i + k - 1],
                recv_sem=recv_sems.at[7 * i + k - 1], device_id=peer, device_id_type=pl.DeviceIdType.MESH)

        order = [2, 4, 6, 3, 5, 7, 1]
        copies = [copy(i, k, _flip(me, k)) for k in order for i in range(n)]
        for cp in copies:
            cp.start()
        for k in order:
            peer = _flip(me, k)
            for i in range(n):
                pltpu.make_async_remote_copy(
                    src_ref=x_refs[i].at[my_slot], dst_ref=out_refs[i].at[_slot(peer)], send_sem=send_sems.at[7 * i + k - 1],
                    recv_sem=recv_sems.at[7 * i + k - 1], device_id=peer, device_id_type=pl.DeviceIdType.MESH).wait_recv()
        for cp in copies:
            cp.wait_send()
        for cp in mine:
            cp.wait()

    return _comm_call(body, name, blocks, [jax.ShapeDtypeStruct(b.shape, b.dtype) for b in blocks])


def _pack_rows(n):
    rows = -(-n // PACK_COLS)
    return -(-rows // 16) * 16


def _pack_flat(pieces, dtype):
    flat = jnp.concatenate([p.reshape(-1).astype(dtype) for p in pieces])
    rows = _pack_rows(flat.shape[0])
    return jnp.pad(flat, (0, rows * PACK_COLS - flat.shape[0])).reshape(rows, PACK_COLS)


def _unpack_flat(buf, shapes):
    lead = buf.shape[:-2]
    flat = buf.reshape(lead + (-1,))
    out, off = [], 0
    for sh in shapes:
        n = int(np.prod(sh))
        out.append(flat[..., off:off + n].reshape(lead + tuple(sh)))
        off += n
    return out


def _merge(gathered, axis):
    if axis == 0:
        return gathered.reshape(-1, gathered.shape[2])
    return jnp.transpose(gathered, (1, 0, 2)).reshape(gathered.shape[1], -1)


def _split(full, axis):
    a, b = full.shape
    if axis == 0:
        return full.reshape(N_DEV, a // N_DEV, b)
    return jnp.transpose(full.reshape(a, N_DEV, b // N_DEV), (1, 0, 2))


SHARDED = {
    "xq_w": 0, "xkv_w": 0, "xo_w": 1, "ffn_in_w": 1, "ffn_conv_w": 1, "ffn_out_w": 0,
    "hyb_in_w": 1, "hyb_out_w": 0, "gdn_conv_w": 1, "lru_conv_w": 1, "swa_in_w": 1, "swa_out_w": 0,
}
F32_SHIPPED = ("ffn_conv_w", "gdn_conv_w", "lru_conv_w")
TRANSPOSED = ("xo_w", "ffn_in_w", "hyb_in_w", "swa_in_w")
COMMON = ("xq_w", "xkv_w", "xo_w", "ffn_in_w", "ffn_conv_w", "ffn_out_w")
EVEN = ("hyb_in_w", "hyb_out_w", "gdn_conv_w", "lru_conv_w")
ODD = ("swa_in_w", "swa_out_w")
WEIGHTS = ["norm_mix", "norm_cross", "norm_mem", "norm_ffn", "xq_w", "xkv_w", "xo_w", "xq_norm", "xk_norm", "ffn_in_w",
           "ffn_conv_w", "ffn_conv_b", "ffn_out_w", "hyb_in_w", "hyb_out_w", "gdn_conv_w", "gdn_a_log", "gdn_dt_bias",
           "gdn_norm", "lru_conv_w", "lru_conv_b", "lru_wr", "lru_br", "lru_wi", "lru_bi", "lru_lambda", "swa_in_w",
           "swa_out_w", "swa_q_norm", "swa_k_norm", "swa_sinks"]
REPLICATED = [n for n in WEIGHTS if n not in SHARDED]


def _layer_names(l):
    return COMMON + (EVEN if l % 2 == 0 else ODD)


PER_LAYER = COMMON + ("norm_mix", "norm_cross", "norm_mem", "norm_ffn", "xq_norm", "xk_norm", "ffn_conv_b")


def _layer_index(name, l):
    return l if name in PER_LAYER else l // 2


def _working(name, shard):
    return shard.T if name in TRANSPOSED else shard


def _gather_layer(w, l):
    names = _layer_names(l)
    pieces = []
    for n in names:
        sh = _working(n, w[n][_layer_index(n, l)])
        pieces.append(sh if n in F32_SHIPPED else sh.astype(BF16))
    gathered = all_gather(pieces, name=f"gather_w{l % 2}")
    full = {}
    for n, g in zip(names, gathered):
        full[n] = _merge(g, 1) if n in F32_SHIPPED else g.reshape(-1, g.shape[2])
    return full


def _hyb_pad(wt):
    d = wt.shape[1]
    return jnp.concatenate([wt[:4096], wt[4112:HYB_IN], wt[4096:4112], jnp.zeros((HYB_PAD - HYB_IN, d), wt.dtype)], axis=0)


def _hyb_unpad(dwt):
    return jnp.concatenate([dwt[:4096], dwt[6144:6160], dwt[4096:6144]], axis=0)


def _pad_lanes(v, n=LANE):
    return jnp.pad(v.reshape(1, -1), ((0, 0), (0, n - v.shape[-1])))


def _layer_fwd(l, x, mem, full, w, cos, sin):
    e = l // 2
    sv = {"x": x}
    g_mix = w["norm_mix"][l:l + 1]
    h1 = norm_fwd(x, g_mix)
    sv["h1"] = h1
    if l % 2 == 0:
        w_in = _hyb_pad(full["hyb_in_w"])
        p = matmul(h1, w_in, "nt", name="mm_hyb_in")
        sc = jnp.concatenate([_pad_lanes(w["gdn_a_log"][e]), _pad_lanes(w["gdn_dt_bias"][e]), jnp.zeros((6, LANE), F32)], 0)
        ng = w["gdn_norm"][e:e + 1]
        oa, states = gdn_fwd(p, full["gdn_conv_w"], sc, ng)
        lru_args = (full["lru_conv_w"], w["lru_conv_b"][e:e + 1], w["lru_wr"][e], w["lru_br"][e:e + 1], w["lru_wi"][e],
                    w["lru_bi"][e:e + 1], w["lru_lambda"][e:e + 1])
        ob, hs = lru_fwd(p, *lru_args)
        w_out = full["hyb_out_w"]
        x1 = matmul(oa, w_out[:GDN_V], "nn", res=x, name="mm_hyb_out_a")
        x1 = matmul(ob, w_out[GDN_V:], "nn", res=x1, name="mm_hyb_out_b")
        sv.update(w_in=w_in, p=p, sc=sc, ng=ng, states=states, lru_args=lru_args, hs=hs, oa=oa, ob=ob)
    else:
        p = matmul(h1, full["swa_in_w"], "nt", name="mm_swa_in")
        swa_args = (cos, sin, w["swa_q_norm"][e:e + 1], w["swa_k_norm"][e:e + 1], _pad_lanes(w["swa_sinks"][e]))
        o = swa_fwd(p, *swa_args)
        x1 = matmul(o, full["swa_out_w"], "nn", res=x, name="mm_swa_out")
        sv.update(p=p, swa_args=swa_args, o=o)
    hc = norm_fwd(x1, w["norm_cross"][l:l + 1])
    memn = norm_fwd(mem, w["norm_mem"][l:l + 1], name="norm_mem_fwd")
    q = matmul(hc, full["xq_w"], "nn", name="mm_xq")
    kv = matmul(memn, full["xkv_w"], "nn", name="mm_xkv")
    qn, kn = w["xq_norm"][l:l + 1], w["xk_norm"][l:l + 1]
    ox = xattn_fwd(q, kv, qn, kn)
    x2 = matmul(ox, full["xo_w"], "nt", res=x1, name="mm_xo")
    hf = norm_fwd(x2, w["norm_ffn"][l:l + 1])
    gu = matmul(hf, full["ffn_in_w"], "nt", name="mm_ffn_in")
    cb = w["ffn_conv_b"][l:l + 1]
    a = ffn_fwd(gu, full["ffn_conv_w"], cb)
    x3 = matmul(a, full["ffn_out_w"], "nn", res=x2, name="mm_ffn_out")
    sv.update(x1=x1, hc=hc, memn=memn, q=q, kv=kv, qn=qn, kn=kn, ox=ox, x2=x2, hf=hf, gu=gu, cb=cb, a=a)
    return x3, sv


def _layer_bwd(l, dx3, mem, full, w, sv):
    e = l // 2
    gs, gr = {}, {}
    f = full["ffn_out_w"].shape[0]
    da = matmul(dx3, full["ffn_out_w"], "nt", name="mm_ffn_out_da")
    gs["ffn_out_w"] = matmul(sv["a"], dx3, "tn", out_dtype=BF16, name="mm_ffn_out_dw")
    dgate, dup, dcw, dcb = ffn_bwd(sv["gu"], full["ffn_conv_w"], sv["cb"], da)
    gs["ffn_conv_w"], gr["ffn_conv_b"] = dcw, dcb
    w_in = full["ffn_in_w"]
    dhf = matmul(dgate, w_in[:f], "nn", name="mm_ffn_in_dh_g")
    dhf = matmul(dup, w_in[f:], "nn", res=dhf, name="mm_ffn_in_dh_u")
    gs["ffn_in_w"] = jnp.concatenate([matmul(dgate, sv["hf"], "tn", out_dtype=BF16, name="mm_ffn_in_dw_g"),
                                      matmul(dup, sv["hf"], "tn", out_dtype=BF16, name="mm_ffn_in_dw_u")], axis=0)
    dx2, gr["norm_ffn"] = norm_bwd(sv["x2"], w["norm_ffn"][l:l + 1], dhf, dx3)
    dox = matmul(dx2, full["xo_w"], "nn", name="mm_xo_do")
    gs["xo_w"] = matmul(dx2, sv["ox"], "tn", out_dtype=BF16, name="mm_xo_dw")
    dq, dk, dv, gr["xq_norm"], gr["xk_norm"] = xattn_bwd(sv["q"], sv["kv"], sv["qn"], sv["kn"], dox)
    dkv = jnp.concatenate([dk, dv], axis=1)
    dhc = matmul(dq, full["xq_w"], "nt", name="mm_xq_dh")
    gs["xq_w"] = matmul(sv["hc"], dq, "tn", out_dtype=BF16, name="mm_xq_dw")
    dmemn = matmul(dkv, full["xkv_w"], "nt", name="mm_xkv_dh")
    gs["xkv_w"] = matmul(sv["memn"], dkv, "tn", out_dtype=BF16, name="mm_xkv_dw")
    gr["norm_mem"] = norm_bwd(mem, w["norm_mem"][l:l + 1], dmemn, name="norm_mem_bwd")
    dx1, gr["norm_cross"] = norm_bwd(sv["x1"], w["norm_cross"][l:l + 1], dhc, dx2)
    if l % 2 == 0:
        w_out = full["hyb_out_w"]
        dmix = matmul(dx1, w_out, "nt", name="mm_hyb_out_dm")
        gs["hyb_out_w"] = jnp.concatenate([matmul(sv["oa"], dx1, "tn", out_dtype=BF16, name="mm_hyb_out_dw_a"),
                                           matmul(sv["ob"], dx1, "tn", out_dtype=BF16, name="mm_hyb_out_dw_b")], axis=0)
        dq_, dk_, dv_, dz, dba, dwq, dwk, dwv, dsc, dng = gdn_bwd(
            sv["p"], full["gdn_conv_w"], sv["sc"], sv["ng"], sv["states"], dmix)
        dlx, dlg, dcw, dcb, dwr, dbr, dwi, dbi, dlam = lru_bwd(sv["p"], *sv["lru_args"], sv["hs"], dmix)
        dba = jnp.sum(dba, axis=0).astype(BF16)
        dp = jnp.concatenate([dq_, dk_, dv_, dz, dlx, dlg, dba], axis=1)
        dh1 = matmul(dp, sv["w_in"], "nn", name="mm_hyb_in_dh")
        gs["hyb_in_w"] = _hyb_unpad(matmul(dp, sv["h1"], "tn", out_dtype=BF16, name="mm_hyb_in_dw"))
        gs["gdn_conv_w"] = jnp.concatenate([dwq, dwk, dwv], axis=1)
        gs["lru_conv_w"] = dcw
        gr.update(gdn_a_log=dsc[0, :GDN_HEADS], gdn_dt_bias=dsc[1, :GDN_HEADS], gdn_norm=dng[0], lru_conv_b=dcb[0],
                  lru_wr=dwr, lru_br=dbr[0], lru_wi=dwi, lru_bi=dbi[0], lru_lambda=dlam[0])
    else:
        do = matmul(dx1, full["swa_out_w"], "nt", name="mm_swa_out_do")
        gs["swa_out_w"] = matmul(sv["o"], dx1, "tn", out_dtype=BF16, name="mm_swa_out_dw")
        dq_, dk_, dv_, dqn, dkn, dsk = swa_bwd(sv["p"], *sv["swa_args"], do)
        dp = jnp.concatenate([dq_, dk_, dv_], axis=1)
        dh1 = matmul(dp, full["swa_in_w"], "nn", name="mm_swa_in_dh")
        gs["swa_in_w"] = matmul(dp, sv["h1"], "tn", out_dtype=BF16, name="mm_swa_in_dw")
        gr.update(swa_q_norm=dqn[0], swa_k_norm=dkn[0], swa_sinks=dsk[0, :SWA_HEADS])
    dx, gr["norm_mix"] = norm_bwd(sv["x"], w["norm_mix"][l:l + 1], dh1, dx1)
    for n in ("norm_ffn", "norm_mem", "norm_cross", "norm_mix", "xq_norm", "xk_norm", "ffn_conv_b"):
        gr[n] = gr[n][0]
    return dx, gs, gr


def _update_layer(l, gs, w, m, v):
    names = _layer_names(l)
    blocks = []
    for n in names:
        g = gs[n]
        blocks.append(_split(g, 1) if n in F32_SHIPPED else g.reshape(N_DEV, g.shape[0] // N_DEV, g.shape[1]))
    parts = all_to_all(blocks, name=f"exchange_g{l % 2}")
    out = {}
    for n, p in zip(names, parts):
        shards = [_working(n, t[n][_layer_index(n, l)]) for t in (w, m, v)]
        quad = adamw(p, *shards, name=f"adamw_{n}")
        out[n] = tuple(_working(n, q) for q in quad)
    return out


def kernel(x, mem, positions, norm_mix, norm_cross, norm_mem, norm_ffn, xq_w, xkv_w, xo_w, xq_norm, xk_norm, ffn_in_w, ffn_conv_w, ffn_conv_b, ffn_out_w, hyb_in_w, hyb_out_w, gdn_conv_w, gdn_a_log, gdn_dt_bias, gdn_norm, lru_conv_w, lru_conv_b, lru_wr, lru_br, lru_wi, lru_bi, lru_lambda, swa_in_w, swa_out_w, swa_q_norm, swa_k_norm, swa_sinks, loss_target, m_norm_mix, m_norm_cross, m_norm_mem, m_norm_ffn, m_xq_w, m_xkv_w, m_xo_w, m_xq_norm, m_xk_norm, m_ffn_in_w, m_ffn_conv_w, m_ffn_conv_b, m_ffn_out_w, m_hyb_in_w, m_hyb_out_w, m_gdn_conv_w, m_gdn_a_log, m_gdn_dt_bias, m_gdn_norm, m_lru_conv_w, m_lru_conv_b, m_lru_wr, m_lru_br, m_lru_wi, m_lru_bi, m_lru_lambda, m_swa_in_w, m_swa_out_w, m_swa_q_norm, m_swa_k_norm, m_swa_sinks, v_norm_mix, v_norm_cross, v_norm_mem, v_norm_ffn, v_xq_w, v_xkv_w, v_xo_w, v_xq_norm, v_xk_norm, v_ffn_in_w, v_ffn_conv_w, v_ffn_conv_b, v_ffn_out_w, v_hyb_in_w, v_hyb_out_w, v_gdn_conv_w, v_gdn_a_log, v_gdn_dt_bias, v_gdn_norm, v_lru_conv_w, v_lru_conv_b, v_lru_wr, v_lru_br, v_lru_wi, v_lru_bi, v_lru_lambda, v_swa_in_w, v_swa_out_w, v_swa_q_norm, v_swa_k_norm, v_swa_sinks):
    args = locals()
    w = {n: args[n] for n in WEIGHTS}
    m = {n: args["m_" + n] for n in WEIGHTS}
    v = {n: args["v_" + n] for n in WEIGHTS}
    depth = norm_mix.shape[0]
    xs = x[0]
    mems = mem[0]
    cos, sin = rope_tables(positions[0].astype(F32).reshape(-1, 1))

    saved, fulls = [], []
    h = xs
    for l in range(depth):
        full = _gather_layer(w, l)
        h, sv = _layer_fwd(l, h, mems, full, w, cos, sin)
        saved.append(sv)
        fulls.append(full)
    dy, lpart = loss_grad(h, loss_target[0])
    loss = lax.psum(lpart[0, 0], ("x", "y", "c"))

    res = {}
    rep = {n: [None] * w[n].shape[0] for n in REPLICATED}
    dh = dy
    for l in reversed(range(depth)):
        dh, gs, gr = _layer_bwd(l, dh, mems, fulls[l], w, saved[l])
        for n, g in gr.items():
            rep[n][_layer_index(n, l)] = g
        for n, quad in _update_layer(l, gs, w, m, v).items():
            res.setdefault(n, [None] * w[n].shape[0])[_layer_index(n, l)] = quad
    grad_x = dh[None]

    rep_shapes = [w[n].shape for n in REPLICATED]
    local = _pack_flat([jnp.stack(rep[n]).reshape(w[n].shape) for n in REPLICATED], F32)
    parts = all_gather([local], name="gather_rep")[0]
    outs = adamw(parts, *[_pack_flat([t[n] for n in REPLICATED], F32) for t in (w, m, v)], name="adamw_rep")
    per = [_unpack_flat(o, rep_shapes) for o in outs]
    quads = {n: tuple(per[k][i] for k in range(4)) for i, n in enumerate(REPLICATED)}
    for n in SHARDED:
        quads[n] = tuple(jnp.stack([res[n][i][k] for i in range(w[n].shape[0])]) for k in range(4))
    return (loss, grad_x, *[quads[n][0] for n in WEIGHTS], *[quads[n][1] for n in WEIGHTS],
            *[quads[n][2] for n in WEIGHTS], *[quads[n][3] for n in WEIGHTS])
```
